```python
import math
import jax, jax.numpy as jnp
from jax import lax
import numpy as np

D_MODEL = 1024
BATCH = 4
SEQ = 8192
DEPTH = 1
DEC_BATCH = 16
DEC_SEQ = 4096
PAST_LEN = 128

GRID_W = 64
N_Q_HEADS = 8
N_KV_HEADS = 2
HEAD_DIM = 64
GQA_GROUP = N_Q_HEADS // N_KV_HEADS
ROPE_HALF = HEAD_DIM // 2
ROPE_THETA = 10000.0
Q_BLOCK = 128
N_DN_HEADS = 4
DN_HEAD_DIM = 128
DN_CONV_K = 3
DN_CHUNK = 64
D_FF = 4 * D_MODEL
EPS = 1e-6

ATT_Q = N_Q_HEADS * HEAD_DIM
ATT_KV = N_KV_HEADS * HEAD_DIM
DN_W = N_DN_HEADS * DN_HEAD_DIM
PART_SIZES = [ATT_Q, ATT_KV, ATT_KV, 3 * DN_W, DN_W, 2 * N_DN_HEADS, 2 * N_DN_HEADS, 2 * D_MODEL]
PROJ_WIDTH = sum(PART_SIZES)
SPLIT_IDX = [int(v) for v in np.cumsum(PART_SIZES[:-1])]

kernel_name = "hybrid_gqa_gdn_bidir_encoder"


def rms_norm(x, g):
    xf = x.astype(jnp.float32)
    y = xf * lax.rsqrt(jnp.mean(xf * xf, axis=-1, keepdims=True) + EPS) * g.astype(jnp.float32)
    return y.astype(x.dtype)


def axial_rope_tables(T):
    rows = T // GRID_W
    row_ids = jnp.repeat(jnp.arange(rows, dtype=jnp.float32), GRID_W)
    col_ids = jnp.tile(jnp.arange(GRID_W, dtype=jnp.float32), rows)
    inv_freq = ROPE_THETA ** (-jnp.arange(0, ROPE_HALF, 2, dtype=jnp.float32) / ROPE_HALF)
    ang_r = row_ids[:, None] * inv_freq
    ang_c = col_ids[:, None] * inv_freq
    return jnp.cos(ang_r), jnp.sin(ang_r), jnp.cos(ang_c), jnp.sin(ang_c)


def rope_rotate(x, cos, sin):
    m = x.shape[-1] // 2
    x1, x2 = x[..., :m], x[..., m:]
    c, s = cos[:, None, :], sin[:, None, :]
    return jnp.concatenate([x1 * c - x2 * s, x2 * c + x1 * s], axis=-1)


def apply_axial_rope(x, tables):
    cos_r, sin_r, cos_c, sin_c = tables
    xf = x.astype(jnp.float32)
    out = jnp.concatenate([rope_rotate(xf[..., :ROPE_HALF], cos_r, sin_r),
                           rope_rotate(xf[..., ROPE_HALF:], cos_c, sin_c)], axis=-1)
    return out.astype(x.dtype)


def blocked_gqa_attention(q, k, v):
    B, T = q.shape[0], q.shape[1]
    nb = T // Q_BLOCK
    scale = HEAD_DIM ** -0.5
    qb = q.reshape(B, nb, Q_BLOCK, N_KV_HEADS, GQA_GROUP, HEAD_DIM).transpose(1, 0, 2, 3, 4, 5)

    def one_block(q_blk):
        s = jnp.einsum('bqkgd,bskd->bkgqs', q_blk, k).astype(jnp.float32) * scale
        p = jax.nn.softmax(s, axis=-1).astype(v.dtype)
        return jnp.einsum('bkgqs,bskd->bqkgd', p, v)

    o = lax.map(one_block, qb)
    return o.transpose(1, 0, 2, 3, 4, 5).reshape(B, T, ATT_Q)


def gated_delta_rule_chunked(q, k, v, g, beta):
    B, T, H, DK = q.shape
    DV = v.shape[-1]
    n = T // DN_CHUNK

    def chunks(x):
        return x.reshape(B, n, DN_CHUNK, H, x.shape[-1]).transpose(1, 0, 3, 2, 4)

    qc = chunks(q) * (DK ** -0.5)
    kc = chunks(k)
    vc = chunks(v)
    gc = jnp.cumsum(chunks(g[..., None])[..., 0], axis=-1)
    bc = chunks(beta[..., None])[..., 0]
    idx = jnp.arange(DN_CHUNK)
    incl = idx[:, None] >= idx[None, :]
    strict = idx[:, None] > idx[None, :]
    decay = jnp.exp(jnp.where(incl, gc[..., :, None] - gc[..., None, :], -jnp.inf))
    kk = jnp.einsum('nbhcd,nbhed->nbhce', kc, kc)
    lower = jnp.where(strict, bc[..., :, None] * kk * decay, 0.0)
    eye = jnp.eye(DN_CHUNK, dtype=jnp.float32)
    tinv = lax.linalg.triangular_solve(lower + eye, jnp.broadcast_to(eye, lower.shape),
                                       left_side=True, lower=True, unit_diagonal=True)
    u = tinv @ (vc * bc[..., None])
    w = tinv @ (kc * (bc * jnp.exp(gc))[..., None])
    qk = jnp.einsum('nbhcd,nbhed->nbhce', qc, kc) * decay
    q_dec = qc * jnp.exp(gc)[..., None]
    g_last = gc[..., -1]
    k_dec = kc * jnp.exp(g_last[..., None] - gc)[..., None]

    def step(state, xs):
        u_i, w_i, qk_i, q_i, k_i, gl_i = xs
        v_new = u_i - w_i @ state
        o_i = q_i @ state + qk_i @ v_new
        state = state * jnp.exp(gl_i)[..., None, None] + jnp.einsum('bhcd,bhce->bhde', k_i, v_new)
        return state, o_i

    state0 = jnp.zeros((B, H, DK, DV), jnp.float32)
    _, o = lax.scan(step, state0, (u, w, qk, q_dec, k_dec, g_last))
    return o.transpose(1, 0, 3, 2, 4).reshape(B, T, H, DV)


def l2_normalise(x):
    return x * lax.rsqrt(jnp.sum(x * x, axis=-1, keepdims=True) + EPS)


def token_mixer(h, w_in, q_norm_g, k_norm_g, conv_w, A_log, dt_bias, dn_norm_g,
                w_attn_branch, w_dn_branch, w_out):
    B, T, _ = h.shape
    proj = h @ w_in
    aq, ak, av, dqkv, dz, dbeta, da, gates = jnp.split(proj, SPLIT_IDX, axis=-1)

    tables = axial_rope_tables(T)
    aq = apply_axial_rope(rms_norm(aq.reshape(B, T, N_Q_HEADS, HEAD_DIM), q_norm_g), tables)
    ak = apply_axial_rope(rms_norm(ak.reshape(B, T, N_KV_HEADS, HEAD_DIM), k_norm_g), tables)
    av = av.reshape(B, T, N_KV_HEADS, HEAD_DIM)
    attn_out = blocked_gqa_attention(aq, ak, av)

    pad = DN_CONV_K // 2
    dqkv = lax.conv_general_dilated(dqkv, conv_w[:, None, :], window_strides=(1,),
                                    padding=((pad, pad),), dimension_numbers=('NWC', 'WIO', 'NWC'),
                                    feature_group_count=3 * DN_W)
    dqkv = jax.nn.silu(dqkv).astype(jnp.float32)
    dq, dk, dv = jnp.split(dqkv, 3, axis=-1)
    dq = l2_normalise(dq.reshape(B, T, N_DN_HEADS, DN_HEAD_DIM))
    dk = l2_normalise(dk.reshape(B, T, N_DN_HEADS, DN_HEAD_DIM))
    dv = dv.reshape(B, T, N_DN_HEADS, DN_HEAD_DIM)
    beta = jax.nn.sigmoid(dbeta.astype(jnp.float32)).reshape(B, T, 2, N_DN_HEADS)
    g = -jnp.exp(A_log.astype(jnp.float32)) * jax.nn.softplus(
        da.astype(jnp.float32).reshape(B, T, 2, N_DN_HEADS) + dt_bias.astype(jnp.float32))
    flip = lambda a: a[:, ::-1]
    o_fwd = gated_delta_rule_chunked(dq, dk, dv, g[:, :, 0], beta[:, :, 0])
    o_bwd = flip(gated_delta_rule_chunked(flip(dq), flip(dk), flip(dv),
                                          flip(g[:, :, 1]), flip(beta[:, :, 1])))
    o = o_fwd + o_bwd
    z = jax.nn.silu(dz.astype(jnp.float32).reshape(B, T, N_DN_HEADS, DN_HEAD_DIM))
    dn_out = (rms_norm(o, dn_norm_g) * z).reshape(B, T, DN_W).astype(h.dtype)

    g_a, g_b = jnp.split(gates, 2, axis=-1)
    merged = jax.nn.sigmoid(g_a) * (attn_out @ w_attn_branch) + jax.nn.sigmoid(g_b) * (dn_out @ w_dn_branch)
    return merged @ w_out


def trunk(x, ln1_pre_g, w_in, attn_q_norm_g, attn_k_norm_g, dn_conv_w, dn_A_log, dn_dt_bias,
          dn_out_norm_g, w_attn_branch, w_dn_branch, w_out, ln1_post_g, ln2_pre_g,
          w_ff_in, w_ff_out, ln2_post_g):
    for l in range(DEPTH):
        mix = token_mixer(rms_norm(x, ln1_pre_g[l]), w_in[l], attn_q_norm_g[l], attn_k_norm_g[l],
                          dn_conv_w[l], dn_A_log[l], dn_dt_bias[l], dn_out_norm_g[l],
                          w_attn_branch[l], w_dn_branch[l], w_out[l])
        x = x + rms_norm(mix, ln1_post_g[l])
        hid = jnp.square(jax.nn.relu(rms_norm(x, ln2_pre_g[l]) @ w_ff_in[l]))
        x = x + rms_norm(hid @ w_ff_out[l], ln2_post_g[l])
    return x


def setup_inputs(seed: int = 0) -> dict:
    key = jax.random.key(seed)
    ks = jax.random.split(key, 20)
    f32 = jnp.float32

    def nrm(k, shape, fan_in):
        return jax.random.normal(k, shape, f32) * fan_in ** -0.5

    def gain(k, shape):
        return 1.0 + 0.05 * jax.random.normal(k, shape, f32)

    dt = jnp.exp(jax.random.uniform(ks[8], (DEPTH, 2, N_DN_HEADS), f32,
                                    minval=math.log(1e-3), maxval=math.log(1e-1)))
    return {
        "x_prompt": jax.random.normal(ks[0], (BATCH, SEQ, D_MODEL), f32),
        "x_sample": jax.random.normal(ks[1], (DEC_BATCH, DEC_SEQ, D_MODEL), f32),
        "ln1_pre_g": gain(ks[2], (DEPTH, D_MODEL)),
        "w_in": nrm(ks[3], (DEPTH, D_MODEL, PROJ_WIDTH), D_MODEL),
        "attn_q_norm_g": gain(ks[4], (DEPTH, HEAD_DIM)),
        "attn_k_norm_g": gain(ks[5], (DEPTH, HEAD_DIM)),
        "dn_conv_w": nrm(ks[6], (DEPTH, DN_CONV_K, 3 * DN_W), DN_CONV_K),
        "dn_A_log": jnp.log(jax.random.uniform(ks[7], (DEPTH, 2, N_DN_HEADS), f32, minval=1.0, maxval=16.0)),
        "dn_dt_bias": dt + jnp.log(-jnp.expm1(-dt)),
        "dn_out_norm_g": gain(ks[9], (DEPTH, DN_HEAD_DIM)),
        "w_attn_branch": nrm(ks[10], (DEPTH, ATT_Q, D_MODEL), ATT_Q),
        "w_dn_branch": nrm(ks[11], (DEPTH, DN_W, D_MODEL), DN_W),
        "w_out": nrm(ks[12], (DEPTH, D_MODEL, D_MODEL), D_MODEL),
        "ln1_post_g": gain(ks[13], (DEPTH, D_MODEL)),
        "ln2_pre_g": gain(ks[14], (DEPTH, D_MODEL)),
        "w_ff_in": nrm(ks[15], (DEPTH, D_MODEL, D_FF), D_MODEL),
        "w_ff_out": nrm(ks[16], (DEPTH, D_FF, D_MODEL), D_FF),
        "ln2_post_g": gain(ks[17], (DEPTH, D_MODEL)),
    }


def reference(x_prompt, x_sample, ln1_pre_g, w_in, attn_q_norm_g, attn_k_norm_g, dn_conv_w,
              dn_A_log, dn_dt_bias, dn_out_norm_g, w_attn_branch, w_dn_branch, w_out,
              ln1_post_g, ln2_pre_g, w_ff_in, w_ff_out, ln2_post_g):
    y_prompt = trunk(x_prompt, ln1_pre_g, w_in, attn_q_norm_g, attn_k_norm_g, dn_conv_w, dn_A_log,
                     dn_dt_bias, dn_out_norm_g, w_attn_branch, w_dn_branch, w_out, ln1_post_g,
                     ln2_pre_g, w_ff_in, w_ff_out, ln2_post_g)
    y_sample = trunk(x_sample, ln1_pre_g, w_in, attn_q_norm_g, attn_k_norm_g, dn_conv_w, dn_A_log,
                     dn_dt_bias, dn_out_norm_g, w_attn_branch, w_dn_branch, w_out, ln1_post_g,
                     ln2_pre_g, w_ff_in, w_ff_out, ln2_post_g)
    return (y_prompt, y_sample)
```

```python
import functools
import math

import jax
import jax.numpy as jnp
from jax import lax
from jax.experimental import pallas as pl
from jax.experimental.pallas import tpu as pltpu

D_MODEL = 1024
GRID_W = 64
N_Q_HEADS = 8
N_KV_HEADS = 2
HEAD_DIM = 64
GQA_GROUP = N_Q_HEADS // N_KV_HEADS
ROPE_HALF = HEAD_DIM // 2
ROPE_FREQS = ROPE_HALF // 2
ROPE_THETA = 10000.0
N_DN_HEADS = 4
DN_HEAD_DIM = 128
DN_CHUNK = 64
D_FF = 4 * D_MODEL
EPS = 1e-6
ATT_Q = N_Q_HEADS * HEAD_DIM
ATT_KV = N_KV_HEADS * HEAD_DIM
DN_W = N_DN_HEADS * DN_HEAD_DIM

LANES = 128
SUBLANES = 8
VMEM_LIMIT_BYTES = 56 * 1024 * 1024

C_Q = 0
C_K = C_Q + ATT_Q
C_V = C_K + ATT_KV
C_D = C_V + ATT_KV
C_Z = C_D + 3 * DN_W
C_G = C_Z + DN_W
C_B = C_G + 2 * D_MODEL
PACK_W = C_B + LANES

PROJ_TM = 256
OUT_TM = 256
ATT_TQ = 128
ATT_KVT = 256
DN_BLK = 4 * DN_CHUNK

F32 = jnp.float32
BF16 = jnp.bfloat16
NEG_BIG = -1e30


def _rms(x, g):
    ms = jnp.mean(x * x, axis=-1, keepdims=True)
    return x * lax.rsqrt(ms + EPS) * g


def _sigmoid(x):
    return 1.0 / (1.0 + jnp.exp(-x))


def _dot(a, b):
    return jnp.dot(a, b, preferred_element_type=F32)


def _dot_nt(a, b):
    return lax.dot_general(a, b, (((1,), (1,)), ((), ())), preferred_element_type=F32)


def _norm_rope_t(xt, gain_t, rope_t):
    ms = jnp.mean(xt * xt, axis=0, keepdims=True)
    xt = xt * lax.rsqrt(ms + EPS) * gain_t
    f = ROPE_FREQS
    x1r, x2r, x1c, x2c = xt[0:f], xt[f:2 * f], xt[2 * f:3 * f], xt[3 * f:4 * f]
    cr, sr, cc, sc = rope_t[0:f], rope_t[f:2 * f], rope_t[2 * f:3 * f], rope_t[3 * f:4 * f]
    return jnp.concatenate([x1r * cr - x2r * sr, x2r * cr + x1r * sr,
                            x1c * cc - x2c * sc, x2c * cc + x1c * sc], axis=0)


def _proj_kernel(x_ref, xp_ref, xn_ref, g1_ref, w_ref, rope_ref, qg_ref, kg_ref, conv_ref,
                 alog_ref, dtb_ref,
                 qt_ref, k_ref, vt_ref, dq_ref, dk_ref, dv_ref, sz_ref, gate_ref, bg_ref, bgt_ref,
                 *, tiles_per_seq):
    tm = x_ref.shape[0]
    ti = pl.program_id(0) % tiles_per_seq
    g1 = g1_ref[...]
    xb = _rms(x_ref[...], g1).astype(BF16)

    rope_t = rope_ref[...]
    yq_t = _dot(xb, w_ref[:, C_Q:C_Q + ATT_Q]).T
    q_scale = (HEAD_DIM ** -0.5) * math.log2(math.e)
    qg = qg_ref[...]
    for h in range(N_Q_HEADS):
        qh = _norm_rope_t(yq_t[h * HEAD_DIM:(h + 1) * HEAD_DIM], qg, rope_t)
        qt_ref[0, h] = (qh * q_scale).astype(BF16)

    yk_t = _dot(xb, w_ref[:, C_K:C_K + ATT_KV]).T
    kg = kg_ref[...]
    k_t = jnp.concatenate([_norm_rope_t(yk_t[h * HEAD_DIM:(h + 1) * HEAD_DIM], kg, rope_t)
                           for h in range(N_KV_HEADS)], axis=0)
    k_ref[0] = k_t.T.astype(BF16)
    yv_t = _dot(xb, w_ref[:, C_V:C_V + ATT_KV]).T.astype(BF16)
    for c in range(tm // ATT_KVT):
        vt_ref[0, c] = yv_t[:, c * ATT_KVT:(c + 1) * ATT_KVT]

    wd = w_ref[:, C_D:C_D + 3 * DN_W]
    y = _dot(xb, wd)
    yp = _dot(_rms(xp_ref[...], g1).astype(BF16), wd)[SUBLANES - 1:SUBLANES]
    yn = _dot(_rms(xn_ref[...], g1).astype(BF16), wd)[0:1]
    yp = jnp.where(ti == 0, 0.0, yp)
    yn = jnp.where(ti == tiles_per_seq - 1, 0.0, yn)
    row = lax.broadcasted_iota(jnp.int32, (tm, 1), 0)
    y_prev = jnp.where(row == 0, yp, pltpu.roll(y, 1, axis=0))
    y_next = jnp.where(row == tm - 1, yn, pltpu.roll(y, tm - 1, axis=0))
    cw = conv_ref[...]
    c = cw[0:1] * y_prev + cw[1:2] * y + cw[2:3] * y_next
    s = c * _sigmoid(c)
    for part, ref, scale in ((0, dq_ref, DN_HEAD_DIM ** -0.5), (1, dk_ref, 1.0)):
        outs = []
        for h in range(N_DN_HEADS):
            lo = part * DN_W + h * DN_HEAD_DIM
            xh = s[:, lo:lo + DN_HEAD_DIM]
            ss = jnp.sum(xh * xh, axis=-1, keepdims=True)
            outs.append(xh * (lax.rsqrt(ss + EPS) * scale))
        ref[...] = jnp.concatenate(outs, axis=1).astype(BF16)
    dv_ref[...] = s[:, 2 * DN_W:3 * DN_W].astype(BF16)

    z = _dot(xb, w_ref[:, C_Z:C_Z + DN_W])
    sz_ref[...] = (z * _sigmoid(z)).astype(BF16)
    gate_ref[...] = _sigmoid(_dot(xb, w_ref[:, C_G:C_G + 2 * D_MODEL])).astype(BF16)

    yb = _dot(xb, w_ref[:, C_B:C_B + LANES])
    lane = lax.broadcasted_iota(jnp.int32, (1, LANES), 1)
    t = yb + dtb_ref[...]
    softplus = jnp.maximum(t, 0.0) + jnp.log1p(jnp.exp(-jnp.abs(t)))
    bg = jnp.where(lane < 2 * N_DN_HEADS, _sigmoid(yb), -jnp.exp(alog_ref[...]) * softplus)
    bg = jnp.where(lane < 4 * N_DN_HEADS, bg, 0.0)
    bg_ref[...] = bg
    bgt_ref[...] = bg.T[0:4 * N_DN_HEADS]


def _proj_call(x2d, seq_len, g1, w_pack, rope_t, qg_t, kg_t, conv_w, alog_row, dtb_row):
    n = x2d.shape[0]
    tm = PROJ_TM
    batch = n // seq_len
    tps = seq_len // tm
    hb = tm // SUBLANES
    n8 = n // SUBLANES
    const = lambda i: (0, 0)
    tok = lambda i: (i, 0)
    in_specs = [
        pl.BlockSpec((tm, D_MODEL), tok),
        pl.BlockSpec((SUBLANES, D_MODEL), lambda i: (jnp.maximum(i * hb - 1, 0), 0)),
        pl.BlockSpec((SUBLANES, D_MODEL), lambda i: (jnp.minimum((i + 1) * hb, n8 - 1), 0)),
        pl.BlockSpec((1, D_MODEL), const),
        pl.BlockSpec((D_MODEL, PACK_W), const),
        pl.BlockSpec((HEAD_DIM, tm), lambda i: (0, i % tps)),
        pl.BlockSpec((HEAD_DIM, tm), const),
        pl.BlockSpec((HEAD_DIM, tm), const),
        pl.BlockSpec((3, 3 * DN_W), const),
        pl.BlockSpec((1, LANES), const),
        pl.BlockSpec((1, LANES), const),
    ]
    out_shape = (
        jax.ShapeDtypeStruct((batch, N_Q_HEADS, HEAD_DIM, seq_len), BF16),
        jax.ShapeDtypeStruct((batch, seq_len, ATT_KV), BF16),
        jax.ShapeDtypeStruct((batch, seq_len // ATT_KVT, ATT_KV, ATT_KVT), BF16),
        jax.ShapeDtypeStruct((n, DN_W), BF16),
        jax.ShapeDtypeStruct((n, DN_W), BF16),
        jax.ShapeDtypeStruct((n, DN_W), BF16),
        jax.ShapeDtypeStruct((n, DN_W), BF16),
        jax.ShapeDtypeStruct((n, 2 * D_MODEL), BF16),
        jax.ShapeDtypeStruct((n, LANES), F32),
        jax.ShapeDtypeStruct((4 * N_DN_HEADS, n), F32),
    )
    kvc = tm // ATT_KVT
    out_specs = (
        pl.BlockSpec((1, N_Q_HEADS, HEAD_DIM, tm), lambda i: (i // tps, 0, 0, i % tps)),
        pl.BlockSpec((1, tm, ATT_KV), lambda i: (i // tps, i % tps, 0)),
        pl.BlockSpec((1, kvc, ATT_KV, ATT_KVT), lambda i: (i // tps, i % tps, 0, 0)),
        pl.BlockSpec((tm, DN_W), tok),
        pl.BlockSpec((tm, DN_W), tok),
        pl.BlockSpec((tm, DN_W), tok),
        pl.BlockSpec((tm, DN_W), tok),
        pl.BlockSpec((tm, 2 * D_MODEL), tok),
        pl.BlockSpec((tm, LANES), tok),
        pl.BlockSpec((4 * N_DN_HEADS, tm), lambda i: (0, i)),
    )
    return pl.pallas_call(
        functools.partial(_proj_kernel, tiles_per_seq=tps),
        grid=(n // tm,),
        in_specs=in_specs,
        out_specs=out_specs,
        out_shape=out_shape,
        compiler_params=pltpu.CompilerParams(dimension_semantics=("arbitrary",),
                                             vmem_limit_bytes=VMEM_LIMIT_BYTES),
        name="proj",
    )(x2d, x2d, x2d, g1, w_pack, rope_t, qg_t, kg_t, conv_w, alog_row, dtb_row)


def _attn_kernel(q_ref, k_ref, v_ref, o_ref, *, n_kv_tiles):
    kvh = pl.program_id(1)
    tq = q_ref.shape[3]
    nq = GQA_GROUP * tq
    q4 = q_ref[0]
    qcat = jnp.concatenate([q4[g] for g in range(GQA_GROUP)], axis=1).astype(F32)
    zero = jnp.zeros_like(qcat)
    qpad = jnp.where(kvh == 0, jnp.concatenate([qcat, zero], axis=0),
                     jnp.concatenate([zero, qcat], axis=0)).astype(BF16)

    def body(j, carry):
        m, l, acc = carry
        kt = k_ref[0, pl.ds(pl.multiple_of(j * ATT_KVT, ATT_KVT), ATT_KVT), :]
        s = _dot(kt, qpad)
        m_new = jnp.maximum(m, jnp.max(s, axis=0, keepdims=True))
        alpha = jnp.exp2(m - m_new)
        p = jnp.exp2(s - m_new)
        l = alpha * l + jnp.sum(p, axis=0, keepdims=True)
        acc = alpha * acc + _dot(v_ref[0, j], p.astype(BF16))
        return m_new, l, acc

    init = (jnp.full((1, nq), NEG_BIG, F32), jnp.zeros((1, nq), F32), jnp.zeros((HEAD_DIM, nq), F32))
    _, l, acc = lax.fori_loop(0, n_kv_tiles, body, init)
    out = acc * (1.0 / l)
    out = jnp.concatenate([out[:, g * tq:(g + 1) * tq] for g in range(GQA_GROUP)], axis=0)
    o_ref[0] = out.T.astype(BF16)


def _attn_call(qt, k, vt):
    batch, _, _, seq_len = qt.shape
    tq = ATT_TQ
    nkv = seq_len // ATT_KVT
    return pl.pallas_call(
        functools.partial(_attn_kernel, n_kv_tiles=nkv),
        grid=(batch, N_KV_HEADS, seq_len // tq),
        in_specs=[
            pl.BlockSpec((1, GQA_GROUP, HEAD_DIM, tq), lambda b, h, i: (b, h, 0, i)),
            pl.BlockSpec((1, seq_len, ATT_KV), lambda b, h, i: (b, 0, 0)),
            pl.BlockSpec((1, nkv, HEAD_DIM, ATT_KVT), lambda b, h, i: (b, 0, h, 0)),
        ],
        out_specs=pl.BlockSpec((1, tq, GQA_GROUP * HEAD_DIM), lambda b, h, i: (b, i, h)),
        out_shape=jax.ShapeDtypeStruct((batch, seq_len, ATT_Q), BF16),
        compiler_params=pltpu.CompilerParams(
            dimension_semantics=("arbitrary", "arbitrary", "arbitrary"),
            vmem_limit_bytes=VMEM_LIMIT_BYTES),
        name="attn",
    )(qt, k, vt)


def _delta_direction(q_ref, k_ref, v_ref, bg_ref, bgt_ref, o_ref, s_ref, reverse):
    blk = DN_BLK
    nch = blk // DN_CHUNK
    d = 1 if reverse else 0
    ri = lax.broadcasted_iota(jnp.int32, (blk, blk), 0)
    ci = lax.broadcasted_iota(jnp.int32, (blk, blk), 1)
    same = (ri // DN_CHUNK) == (ci // DN_CHUNK)
    if reverse:
        incl, strict, incl_t = same & (ri <= ci), same & (ri < ci), same & (ri >= ci)
    else:
        incl, strict, incl_t = same & (ri >= ci), same & (ri > ci), same & (ri <= ci)
    eye = (ri == ci).astype(F32)

    bg = bg_ref[...]
    bgt = bgt_ref[...]
    hi = lax.Precision.HIGHEST
    gc = jnp.dot(incl.astype(F32), bg, precision=hi, preferred_element_type=F32)
    gct = jnp.dot(bgt, incl_t.astype(F32), precision=hi, preferred_element_type=F32)
    last_row = [(c * DN_CHUNK if reverse else (c + 1) * DN_CHUNK - 1) for c in range(nch)]
    order = list(range(nch - 1, -1, -1)) if reverse else list(range(nch))

    for h in range(N_DN_HEADS):
        cb = d * N_DN_HEADS + h
        cg = 2 * N_DN_HEADS + cb
        lo = h * DN_HEAD_DIM
        qh = q_ref[:, lo:lo + DN_HEAD_DIM]
        kh = k_ref[:, lo:lo + DN_HEAD_DIM]
        vh = v_ref[:, lo:lo + DN_HEAD_DIM]
        b_col = bg[:, cb:cb + 1]
        g_col = gc[:, cg:cg + 1]
        g_row = gct[cg:cg + 1, :]
        decay = jnp.exp(jnp.where(incl, g_col - g_row, NEG_BIG))
        kk = _dot_nt(kh, kh)
        qk = _dot_nt(qh, kh) * decay
        low = jnp.where(strict, b_col * kk * decay, 0.0)
        lb = low.astype(BF16)
        xp = _dot(lb, lb)
        tinv = eye - low
        n_sq = int(math.log2(DN_CHUNK)) - 1
        for it in range(n_sq):
            xb = xp.astype(BF16)
            tinv = tinv + _dot(tinv.astype(BF16), xb)
            if it < n_sq - 1:
                xp = _dot(xb, xb)
        e_col = jnp.exp(g_col)
        kf = kh.astype(F32)
        rhs = jnp.concatenate([vh.astype(F32) * b_col, kf * (b_col * e_col)], axis=1).astype(BF16)
        uw = _dot(tinv.astype(BF16), rhs)
        u, w = uw[:, :DN_HEAD_DIM], uw[:, DN_HEAD_DIM:].astype(BF16)
        q_dec = (qh.astype(F32) * e_col).astype(BF16)
        gl_col = jnp.concatenate(
            [jnp.broadcast_to(g_col[last_row[c]:last_row[c] + 1], (DN_CHUNK, 1)) for c in range(nch)], axis=0)
        kd_t = (kf * jnp.exp(gl_col - g_col)).T.astype(BF16)

        state = s_ref[d * N_DN_HEADS + h]
        v_new = [None] * nch
        o_q = [None] * nch
        zeros_c = jnp.zeros((DN_CHUNK, DN_HEAD_DIM), BF16)
        for c in order:
            r0 = c * DN_CHUNK
            sb = state.astype(BF16)
            vn = u[r0:r0 + DN_CHUNK] - _dot(w[r0:r0 + DN_CHUNK], sb)
            o_q[c] = _dot(q_dec[r0:r0 + DN_CHUNK], sb)
            v_new[c] = vn
            vn_full = jnp.concatenate([vn.astype(BF16) if cc == c else zeros_c for cc in range(nch)], axis=0)
            state = state * jnp.exp(g_col[last_row[c]:last_row[c] + 1]) + _dot(kd_t, vn_full)
        s_ref[d * N_DN_HEADS + h] = state
        vn_all = jnp.concatenate(v_new, axis=0).astype(BF16)
        o = jnp.concatenate(o_q, axis=0) + _dot(qk.astype(BF16), vn_all)
        o_ref[:, lo:lo + DN_HEAD_DIM] = o.astype(BF16)


def _delta_kernel(qf_ref, kf_ref, vf_ref, bgf_ref, bgtf_ref, qb_ref, kb_ref, vb_ref, bgb_ref, bgtb_ref,
                  of_ref, ob_ref, s_ref):
    @pl.when(pl.program_id(1) == 0)
    def _():
        s_ref[...] = jnp.zeros_like(s_ref)

    _delta_direction(qf_ref, kf_ref, vf_ref, bgf_ref, bgtf_ref, of_ref, s_ref, False)
    _delta_direction(qb_ref, kb_ref, vb_ref, bgb_ref, bgtb_ref, ob_ref, s_ref, True)


def _delta_call(dq, dk, dv, bg, bgt, seq_len):
    n = dq.shape[0]
    batch = n // seq_len
    nb = seq_len // DN_BLK
    fwd = lambda b, i: (b * nb + i, 0)
    bwd = lambda b, i: (b * nb + nb - 1 - i, 0)
    fwd_t = lambda b, i: (0, b * nb + i)
    bwd_t = lambda b, i: (0, b * nb + nb - 1 - i)
    tok = lambda m: pl.BlockSpec((DN_BLK, DN_W), m)
    in_specs = [tok(fwd), tok(fwd), tok(fwd), pl.BlockSpec((DN_BLK, LANES), fwd),
                pl.BlockSpec((4 * N_DN_HEADS, DN_BLK), fwd_t),
                tok(bwd), tok(bwd), tok(bwd), pl.BlockSpec((DN_BLK, LANES), bwd),
                pl.BlockSpec((4 * N_DN_HEADS, DN_BLK), bwd_t)]
    return pl.pallas_call(
        _delta_kernel,
        grid=(batch, nb),
        in_specs=in_specs,
        out_specs=(tok(fwd), tok(bwd)),
        out_shape=(jax.ShapeDtypeStruct((n, DN_W), BF16), jax.ShapeDtypeStruct((n, DN_W), BF16)),
        scratch_shapes=[pltpu.VMEM((2 * N_DN_HEADS, DN_HEAD_DIM, DN_HEAD_DIM), F32)],
        compiler_params=pltpu.CompilerParams(dimension_semantics=("arbitrary", "arbitrary"),
                                             vmem_limit_bytes=VMEM_LIMIT_BYTES),
        name="delta",
    )(dq, dk, dv, bg, bgt, dq, dk, dv, bg, bgt)


def _out_kernel(x_ref, attn_ref, of_ref, ob_ref, sz_ref, gate_ref, dng_ref, wa_ref, wb_ref, wo_ref,
                g1p_ref, g2_ref, w1_ref, w2_ref, g2p_ref, y_ref):
    o = of_ref[...].astype(F32) + ob_ref[...].astype(F32)
    dng = dng_ref[...]
    dn = jnp.concatenate([_rms(o[:, h * DN_HEAD_DIM:(h + 1) * DN_HEAD_DIM], dng)
                          for h in range(N_DN_HEADS)], axis=1)
    dn = (dn * sz_ref[...].astype(F32)).astype(BF16)
    a = _dot(attn_ref[...], wa_ref[...])
    dd = _dot(dn, wb_ref[...])
    gate = gate_ref[...].astype(F32)
    merged = (gate[:, :D_MODEL] * a + gate[:, D_MODEL:] * dd).astype(BF16)
    h1 = x_ref[...] + _rms(_dot(merged, wo_ref[...]), g1p_ref[...])
    hid = _dot(_rms(h1, g2_ref[...]).astype(BF16), w1_ref[...])
    hid = jnp.square(jnp.maximum(hid, 0.0)).astype(BF16)
    y_ref[...] = h1 + _rms(_dot(hid, w2_ref[...]), g2p_ref[...])


def _out_call(x2d, attn2d, o_f, o_b, sz, gate, dng, wa, wb, wo, g1p, g2, w1, w2, g2p):
    n = x2d.shape[0]
    tm = OUT_TM
    tok = lambda w: pl.BlockSpec((tm, w), lambda i: (i, 0))
    full = lambda a: pl.BlockSpec(a.shape, lambda i: (0, 0))
    return pl.pallas_call(
        _out_kernel,
        grid=(n // tm,),
        in_specs=[tok(D_MODEL), tok(ATT_Q), tok(DN_W), tok(DN_W), tok(DN_W), tok(2 * D_MODEL),
                  full(dng), full(wa), full(wb), full(wo), full(g1p), full(g2), full(w1), full(w2), full(g2p)],
        out_specs=tok(D_MODEL),
        out_shape=jax.ShapeDtypeStruct((n, D_MODEL), F32),
        compiler_params=pltpu.CompilerParams(dimension_semantics=("arbitrary",),
                                             vmem_limit_bytes=VMEM_LIMIT_BYTES),
        name="outmlp",
    )(x2d, attn2d, o_f, o_b, sz, gate, dng, wa, wb, wo, g1p, g2, w1, w2, g2p)


def _rope_table_t(seq_len):
    pos = jnp.arange(seq_len, dtype=jnp.int32)
    row_ids = (pos // GRID_W).astype(F32)
    col_ids = (pos % GRID_W).astype(F32)
    inv_freq = ROPE_THETA ** (-jnp.arange(0, ROPE_HALF, 2, dtype=F32) / ROPE_HALF)
    ang_r = inv_freq[:, None] * row_ids[None, :]
    ang_c = inv_freq[:, None] * col_ids[None, :]
    return jnp.concatenate([jnp.cos(ang_r), jnp.sin(ang_r), jnp.cos(ang_c), jnp.sin(ang_c)], axis=0)


def _pack_w_in(w_in):
    aq, ak, av, dqkv, dz, dbeta, da, gates = jnp.split(
        w_in, [ATT_Q, ATT_Q + ATT_KV, ATT_Q + 2 * ATT_KV, C_Z, C_Z + DN_W,
               C_Z + DN_W + 2 * N_DN_HEADS, C_Z + DN_W + 4 * N_DN_HEADS], axis=-1)
    pad = jnp.zeros((D_MODEL, LANES - 4 * N_DN_HEADS), w_in.dtype)
    return jnp.concatenate([aq, ak, av, dqkv, dz, gates, dbeta, da, pad], axis=-1).astype(BF16)


def _lane_row(v):
    v = v.reshape(-1).astype(F32)
    return jnp.zeros((1, LANES), F32).at[0, 2 * N_DN_HEADS:4 * N_DN_HEADS].set(v)


def _layer(x, p):
    batch, seq_len, _ = x.shape
    n = batch * seq_len
    x2d = x.reshape(n, D_MODEL)
    rope_t = _rope_table_t(seq_len)
    qt, k, vt, dq, dk, dv, sz, gate, bg, bgt = _proj_call(
        x2d, seq_len, p["g1"], p["w_pack"], rope_t, p["qg_t"], p["kg_t"], p["conv_w"], p["alog"], p["dtb"])
    attn = _attn_call(qt, k, vt).reshape(n, ATT_Q)
    o_f, o_b = _delta_call(dq, dk, dv, bg, bgt, seq_len)
    y = _out_call(x2d, attn, o_f, o_b, sz, gate, p["dng"], p["wa"], p["wb"], p["wo"],
                  p["g1p"], p["g2"], p["w1"], p["w2"], p["g2p"])
    return y.reshape(batch, seq_len, D_MODEL)


def kernel(x_prompt, x_sample, ln1_pre_g, w_in, attn_q_norm_g, attn_k_norm_g, dn_conv_w, dn_A_log, dn_dt_bias, dn_out_norm_g, w_attn_branch, w_dn_branch, w_out, ln1_post_g, ln2_pre_g, w_ff_in, w_ff_out, ln2_post_g):
    depth = w_in.shape[0]
    outs = []
    for x in (x_prompt, x_sample):
        for l in range(depth):
            p = {
                "g1": ln1_pre_g[l].reshape(1, D_MODEL),
                "w_pack": _pack_w_in(w_in[l]),
                "qg_t": jnp.broadcast_to(attn_q_norm_g[l][:, None], (HEAD_DIM, PROJ_TM)),
                "kg_t": jnp.broadcast_to(attn_k_norm_g[l][:, None], (HEAD_DIM, PROJ_TM)),
                "conv_w": dn_conv_w[l],
                "alog": _lane_row(dn_A_log[l]),
                "dtb": _lane_row(dn_dt_bias[l]),
                "dng": dn_out_norm_g[l].reshape(1, DN_HEAD_DIM),
                "wa": w_attn_branch[l].astype(BF16),
                "wb": w_dn_branch[l].astype(BF16),
                "wo": w_out[l].astype(BF16),
                "g1p": ln1_post_g[l].reshape(1, D_MODEL),
                "g2": ln2_pre_g[l].reshape(1, D_MODEL),
                "w1": w_ff_in[l].astype(BF16),
                "w2": w_ff_out[l].astype(BF16),
                "g2p": ln2_post_g[l].reshape(1, D_MODEL),
            }
            x = _layer(x, p)
        outs.append(x)
    return tuple(outs)
```

```python
import functools
import math

import jax
import jax.numpy as jnp
from jax import lax
from jax.experimental import pallas as pl
from jax.experimental.pallas import tpu as pltpu

D_MODEL = 1024
GRID_W = 64
N_Q_HEADS = 8
N_KV_HEADS = 2
HEAD_DIM = 64
GQA_GROUP = N_Q_HEADS // N_KV_HEADS
ROPE_HALF = HEAD_DIM // 2
ROPE_FREQS = ROPE_HALF // 2
ROPE_THETA = 10000.0
N_DN_HEADS = 4
DN_HEAD_DIM = 128
DN_CHUNK = 64
D_FF = 4 * D_MODEL
EPS = 1e-6
ATT_Q = N_Q_HEADS * HEAD_DIM
ATT_KV = N_KV_HEADS * HEAD_DIM
DN_W = N_DN_HEADS * DN_HEAD_DIM

LANES = 128
SUBLANES = 8
VMEM_LIMIT_BYTES = 56 * 1024 * 1024

C_Q = 0
C_K = C_Q + ATT_Q
C_V = C_K + ATT_KV
C_D = C_V + ATT_KV
C_Z = C_D + 3 * DN_W
C_G = C_Z + DN_W
C_B = C_G + 2 * D_MODEL
PACK_W = C_B + LANES

PROJ_TM = 256
OUT_TM = 256
ATT_TQ = 128
ATT_KVT = 512
ATT_VT = 256
DN_BLK = 4 * DN_CHUNK
DN_DIRS = 2

F32 = jnp.float32
BF16 = jnp.bfloat16
NEG_BIG = -1e30


def _rms(x, g):
    ms = jnp.mean(x * x, axis=-1, keepdims=True)
    return x * lax.rsqrt(ms + EPS) * g


def _sigmoid(x):
    return 1.0 / (1.0 + jnp.exp(-x))


def _dot(a, b):
    return jnp.dot(a, b, preferred_element_type=F32)


def _dot_nt(a, b):
    return lax.dot_general(a, b, (((1,), (1,)), ((), ())), preferred_element_type=F32)


def _norm_rope_t(xt, gain_t, rope_t):
    ms = jnp.mean(xt * xt, axis=0, keepdims=True)
    xt = xt * lax.rsqrt(ms + EPS) * gain_t
    f = ROPE_FREQS
    x1r, x2r, x1c, x2c = xt[0:f], xt[f:2 * f], xt[2 * f:3 * f], xt[3 * f:4 * f]
    cr, sr, cc, sc = rope_t[0:f], rope_t[f:2 * f], rope_t[2 * f:3 * f], rope_t[3 * f:4 * f]
    return jnp.concatenate([x1r * cr - x2r * sr, x2r * cr + x1r * sr,
                            x1c * cc - x2c * sc, x2c * cc + x1c * sc], axis=0)


def _proj_kernel(x_ref, xp_ref, xn_ref, g1_ref, w_ref, rope_ref, qg_ref, kg_ref, conv_ref,
                 alog_ref, dtb_ref,
                 qt_ref, k_ref, vt_ref, dq_ref, dk_ref, dv_ref, sz_ref, gate_ref, bg_ref, bgt_ref,
                 *, tiles_per_seq):
    tm = x_ref.shape[0]
    ti = pl.program_id(0) % tiles_per_seq
    g1 = g1_ref[...]
    xb = _rms(x_ref[...], g1).astype(BF16)

    rope_t = rope_ref[...]
    yq_t = _dot(xb, w_ref[:, C_Q:C_Q + ATT_Q]).T
    q_scale = (HEAD_DIM ** -0.5) * math.log2(math.e)
    qg = qg_ref[...]
    for h in range(N_Q_HEADS):
        qh = _norm_rope_t(yq_t[h * HEAD_DIM:(h + 1) * HEAD_DIM], qg, rope_t)
        qt_ref[0, h] = (qh * q_scale).astype(BF16)

    yk_t = _dot(xb, w_ref[:, C_K:C_K + ATT_KV]).T
    kg = kg_ref[...]
    k_t = jnp.concatenate([_norm_rope_t(yk_t[h * HEAD_DIM:(h + 1) * HEAD_DIM], kg, rope_t)
                           for h in range(N_KV_HEADS)], axis=0)
    k_ref[0] = k_t.T.astype(BF16)
    yv_t = _dot(xb, w_ref[:, C_V:C_V + ATT_KV]).T.astype(BF16)
    for c in range(tm // ATT_VT):
        vt_ref[0, c] = yv_t[:, c * ATT_VT:(c + 1) * ATT_VT]

    wd = w_ref[:, C_D:C_D + 3 * DN_W]
    y = _dot(xb, wd)
    yp = _dot(_rms(xp_ref[...], g1).astype(BF16), wd)[SUBLANES - 1:SUBLANES]
    yn = _dot(_rms(xn_ref[...], g1).astype(BF16), wd)[0:1]
    yp = jnp.where(ti == 0, 0.0, yp)
    yn = jnp.where(ti == tiles_per_seq - 1, 0.0, yn)
    row = lax.broadcasted_iota(jnp.int32, (tm, 1), 0)
    y_prev = jnp.where(row == 0, yp, pltpu.roll(y, 1, axis=0))
    y_next = jnp.where(row == tm - 1, yn, pltpu.roll(y, tm - 1, axis=0))
    cw = conv_ref[...]
    c = cw[0:1] * y_prev + cw[1:2] * y + cw[2:3] * y_next
    s = c * _sigmoid(c)
    for part, ref, scale in ((0, dq_ref, DN_HEAD_DIM ** -0.5), (1, dk_ref, 1.0)):
        outs = []
        for h in range(N_DN_HEADS):
            lo = part * DN_W + h * DN_HEAD_DIM
            xh = s[:, lo:lo + DN_HEAD_DIM]
            ss = jnp.sum(xh * xh, axis=-1, keepdims=True)
            outs.append(xh * (lax.rsqrt(ss + EPS) * scale))
        ref[...] = jnp.concatenate(outs, axis=1).astype(BF16)
    dv_ref[...] = s[:, 2 * DN_W:3 * DN_W].astype(BF16)

    z = _dot(xb, w_ref[:, C_Z:C_Z + DN_W])
    sz_ref[...] = (z * _sigmoid(z)).astype(BF16)
    gate_ref[...] = _sigmoid(_dot(xb, w_ref[:, C_G:C_G + 2 * D_MODEL])).astype(BF16)

    yb = _dot(xb, w_ref[:, C_B:C_B + LANES])
    lane = lax.broadcasted_iota(jnp.int32, (1, LANES), 1)
    t = yb + dtb_ref[...]
    softplus = jnp.maximum(t, 0.0) + jnp.log1p(jnp.exp(-jnp.abs(t)))
    bg = jnp.where(lane < 2 * N_DN_HEADS, _sigmoid(yb), -jnp.exp(alog_ref[...]) * softplus)
    bg = jnp.where(lane < 4 * N_DN_HEADS, bg, 0.0)
    bg_ref[...] = bg
    bgt_ref[...] = bg.T[0:4 * N_DN_HEADS]


def _proj_call(x2d, seq_len, g1, w_pack, rope_t, qg_t, kg_t, conv_w, alog_row, dtb_row):
    n = x2d.shape[0]
    tm = PROJ_TM
    batch = n // seq_len
    tps = seq_len // tm
    hb = tm // SUBLANES
    n8 = n // SUBLANES
    const = lambda i: (0, 0)
    tok = lambda i: (i, 0)
    in_specs = [
        pl.BlockSpec((tm, D_MODEL), tok),
        pl.BlockSpec((SUBLANES, D_MODEL), lambda i: (jnp.maximum(i * hb - 1, 0), 0)),
        pl.BlockSpec((SUBLANES, D_MODEL), lambda i: (jnp.minimum((i + 1) * hb, n8 - 1), 0)),
        pl.BlockSpec((1, D_MODEL), const),
        pl.BlockSpec((D_MODEL, PACK_W), const),
        pl.BlockSpec((HEAD_DIM, tm), lambda i: (0, i % tps)),
        pl.BlockSpec((HEAD_DIM, tm), const),
        pl.BlockSpec((HEAD_DIM, tm), const),
        pl.BlockSpec((3, 3 * DN_W), const),
        pl.BlockSpec((1, LANES), const),
        pl.BlockSpec((1, LANES), const),
    ]
    out_shape = (
        jax.ShapeDtypeStruct((batch, N_Q_HEADS, HEAD_DIM, seq_len), BF16),
        jax.ShapeDtypeStruct((batch, seq_len, ATT_KV), BF16),
        jax.ShapeDtypeStruct((batch, seq_len // ATT_VT, ATT_KV, ATT_VT), BF16),
        jax.ShapeDtypeStruct((n, DN_W), BF16),
        jax.ShapeDtypeStruct((n, DN_W), BF16),
        jax.ShapeDtypeStruct((n, DN_W), BF16),
        jax.ShapeDtypeStruct((n, DN_W), BF16),
        jax.ShapeDtypeStruct((n, 2 * D_MODEL), BF16),
        jax.ShapeDtypeStruct((n, LANES), F32),
        jax.ShapeDtypeStruct((4 * N_DN_HEADS, n), F32),
    )
    kvc = tm // ATT_VT
    out_specs = (
        pl.BlockSpec((1, N_Q_HEADS, HEAD_DIM, tm), lambda i: (i // tps, 0, 0, i % tps)),
        pl.BlockSpec((1, tm, ATT_KV), lambda i: (i // tps, i % tps, 0)),
        pl.BlockSpec((1, kvc, ATT_KV, ATT_VT), lambda i: (i // tps, i % tps, 0, 0)),
        pl.BlockSpec((tm, DN_W), tok),
        pl.BlockSpec((tm, DN_W), tok),
        pl.BlockSpec((tm, DN_W), tok),
        pl.BlockSpec((tm, DN_W), tok),
        pl.BlockSpec((tm, 2 * D_MODEL), tok),
        pl.BlockSpec((tm, LANES), tok),
        pl.BlockSpec((4 * N_DN_HEADS, tm), lambda i: (0, i)),
    )
    return pl.pallas_call(
        functools.partial(_proj_kernel, tiles_per_seq=tps),
        grid=(n // tm,),
        in_specs=in_specs,
        out_specs=out_specs,
        out_shape=out_shape,
        compiler_params=pltpu.CompilerParams(dimension_semantics=("arbitrary",),
                                             vmem_limit_bytes=VMEM_LIMIT_BYTES),
        name="proj",
    )(x2d, x2d, x2d, g1, w_pack, rope_t, qg_t, kg_t, conv_w, alog_row, dtb_row)


def _attn_kernel(q_ref, k_ref, v_ref, o_ref, s_scr, acc_scr, *, n_kv_tiles):
    kvh = pl.program_id(1)
    tq = q_ref.shape[3]
    nq = GQA_GROUP * tq
    q4 = q_ref[0]
    qcat = jnp.concatenate([q4[g] for g in range(GQA_GROUP)], axis=1).astype(F32)
    zero = jnp.zeros_like(qcat)
    qpad = jnp.where(kvh == 0, jnp.concatenate([qcat, zero], axis=0),
                     jnp.concatenate([zero, qcat], axis=0)).astype(BF16)
    vt_per_tile = ATT_KVT // ATT_VT

    def scores(j):
        kt = k_ref[0, pl.ds(pl.multiple_of(j * ATT_KVT, ATT_KVT), ATT_KVT), :]
        return _dot(kt, qpad)

    def update(slot, j, m, l):
        s = s_scr[slot]
        m_new = jnp.maximum(m, jnp.max(s, axis=0, keepdims=True))
        alpha = jnp.exp2(m - m_new)
        p = jnp.exp2(s - m_new)
        l = alpha * l + jnp.sum(p, axis=0, keepdims=True)
        vt = jnp.concatenate([v_ref[0, j * vt_per_tile + c] for c in range(vt_per_tile)], axis=1)
        acc_scr[...] = alpha * acc_scr[...] + _dot(vt, p.astype(BF16))
        return m_new, l

    acc_scr[...] = jnp.zeros_like(acc_scr)
    s_scr[0] = scores(0)

    def body(jj, carry):
        m, l = carry
        j0 = 2 * jj
        s_scr[1] = scores(j0 + 1)
        m, l = update(0, j0, m, l)
        s_scr[0] = scores(jnp.minimum(j0 + 2, n_kv_tiles - 1))
        return update(1, j0 + 1, m, l)

    init = (jnp.full((1, nq), NEG_BIG, F32), jnp.zeros((1, nq), F32))
    _, l = lax.fori_loop(0, n_kv_tiles // 2, body, init)
    out = acc_scr[...] * (1.0 / l)
    out = jnp.concatenate([out[:, g * tq:(g + 1) * tq] for g in range(GQA_GROUP)], axis=0)
    o_ref[0] = out.T.astype(BF16)


def _attn_call(qt, k, vt):
    batch, _, _, seq_len = qt.shape
    tq = ATT_TQ
    nkv = seq_len // ATT_KVT
    assert nkv % 2 == 0
    nq = GQA_GROUP * tq
    return pl.pallas_call(
        functools.partial(_attn_kernel, n_kv_tiles=nkv),
        grid=(batch, N_KV_HEADS, seq_len // tq),
        in_specs=[
            pl.BlockSpec((1, GQA_GROUP, HEAD_DIM, tq), lambda b, h, i: (b, h, 0, i)),
            pl.BlockSpec((1, seq_len, ATT_KV), lambda b, h, i: (b, 0, 0)),
            pl.BlockSpec((1, seq_len // ATT_VT, HEAD_DIM, ATT_VT), lambda b, h, i: (b, 0, h, 0)),
        ],
        out_specs=pl.BlockSpec((1, tq, GQA_GROUP * HEAD_DIM), lambda b, h, i: (b, i, h)),
        out_shape=jax.ShapeDtypeStruct((batch, seq_len, ATT_Q), BF16),
        scratch_shapes=[pltpu.VMEM((2, ATT_KVT, nq), F32), pltpu.VMEM((HEAD_DIM, nq), F32)],
        compiler_params=pltpu.CompilerParams(
            dimension_semantics=("arbitrary", "arbitrary", "arbitrary"),
            vmem_limit_bytes=VMEM_LIMIT_BYTES),
        name="attn",
    )(qt, k, vt)


def _split3(x):
    p1 = x.astype(BF16)
    r1 = x - p1.astype(F32)
    p2 = r1.astype(BF16)
    p3 = (r1 - p2.astype(F32)).astype(BF16)
    return p1, p2, p3


def _delta_kernel(qf_ref, kf_ref, vf_ref, bgf_ref, bgtf_ref, qb_ref, kb_ref, vb_ref, bgb_ref, bgtb_ref,
                  of_ref, ob_ref, s_ref):
    @pl.when(pl.program_id(1) == 0)
    def _():
        s_ref[...] = jnp.zeros_like(s_ref)

    blk = DN_BLK
    nch = blk // DN_CHUNK
    ri = lax.broadcasted_iota(jnp.int32, (blk, blk), 0)
    ci = lax.broadcasted_iota(jnp.int32, (blk, blk), 1)
    same = (ri // DN_CHUNK) == (ci // DN_CHUNK)
    lower_incl, lower_strict = same & (ri >= ci), same & (ri > ci)
    upper_incl, upper_strict = same & (ri <= ci), same & (ri < ci)
    eye = (ri == ci).astype(F32)
    refs = ((qf_ref, kf_ref, vf_ref, bgf_ref, bgtf_ref, of_ref),
            (qb_ref, kb_ref, vb_ref, bgb_ref, bgtb_ref, ob_ref))

    dirs = []
    for d in range(DN_DIRS):
        reverse = d == 1
        incl, strict, incl_t = ((upper_incl, upper_strict, lower_incl) if reverse
                                else (lower_incl, lower_strict, upper_incl))
        bg = refs[d][3][...]
        bgt = refs[d][4][...]
        m_col = jnp.where(incl, 1.0, 0.0).astype(BF16)
        m_row = jnp.where(incl_t, 1.0, 0.0).astype(BF16)
        gc = sum(_dot(m_col, piece) for piece in _split3(bg))
        gct = sum(_dot(piece, m_row) for piece in _split3(bgt))
        last_row = [(c * DN_CHUNK if reverse else (c + 1) * DN_CHUNK - 1) for c in range(nch)]
        order = list(range(nch - 1, -1, -1)) if reverse else list(range(nch))
        dirs.append(dict(incl=incl, strict=strict, bg=bg, gc=gc, gct=gct, last_row=last_row, order=order))

    chains = [(d, h) for d in range(DN_DIRS) for h in range(N_DN_HEADS)]
    nc = len(chains)

    qh, kh, vh, kk, qk = [], [], [], [], []
    for d, h in chains:
        lo = h * DN_HEAD_DIM
        qh.append(refs[d][0][:, lo:lo + DN_HEAD_DIM])
        kh.append(refs[d][1][:, lo:lo + DN_HEAD_DIM])
        vh.append(refs[d][2][:, lo:lo + DN_HEAD_DIM])
    for i in range(nc):
        kk.append(_dot_nt(kh[i], kh[i]))
        qk.append(_dot_nt(qh[i], kh[i]))

    low, qkd, rhs, q_dec, kd_t, g_cols = [], [], [], [], [], []
    for i, (d, h) in enumerate(chains):
        dd = dirs[d]
        cb = d * N_DN_HEADS + h
        cg = 2 * N_DN_HEADS + cb
        b_col = dd["bg"][:, cb:cb + 1]
        g_col = dd["gc"][:, cg:cg + 1]
        g_row = dd["gct"][cg:cg + 1, :]
        decay = jnp.exp(jnp.where(dd["incl"], g_col - g_row, NEG_BIG))
        low.append(jnp.where(dd["strict"], b_col * kk[i] * decay, 0.0))
        qkd.append((qk[i] * decay).astype(BF16))
        e_col = jnp.exp(g_col)
        kf = kh[i].astype(F32)
        rhs.append(jnp.concatenate([vh[i].astype(F32) * b_col, kf * (b_col * e_col)], axis=1).astype(BF16))
        q_dec.append((qh[i].astype(F32) * e_col).astype(BF16))
        gl_col = jnp.concatenate(
            [jnp.broadcast_to(g_col[r:r + 1], (DN_CHUNK, 1)) for r in dd["last_row"]], axis=0)
        kd_t.append((kf * jnp.exp(gl_col - g_col)).T.astype(BF16))
        g_cols.append(g_col)

    lb = [x.astype(BF16) for x in low]
    xp = [_dot(lb[i], lb[i]) for i in range(nc)]
    tinv = [eye - low[i] for i in range(nc)]
    n_sq = int(math.log2(DN_CHUNK)) - 1
    for it in range(n_sq):
        xb = [x.astype(BF16) for x in xp]
        prod = [_dot(tinv[i].astype(BF16), xb[i]) for i in range(nc)]
        if it < n_sq - 1:
            xp = [_dot(xb[i], xb[i]) for i in range(nc)]
        tinv = [tinv[i] + prod[i] for i in range(nc)]

    uw = [_dot(tinv[i].astype(BF16), rhs[i]) for i in range(nc)]
    u = [x[:, :DN_HEAD_DIM] for x in uw]
    w = [x[:, DN_HEAD_DIM:].astype(BF16) for x in uw]

    state = [s_ref[i] for i in range(nc)]
    v_new = [[None] * nch for _ in range(nc)]
    o_q = [[None] * nch for _ in range(nc)]
    zeros_c = jnp.zeros((DN_CHUNK, DN_HEAD_DIM), BF16)
    for step in range(nch):
        sb = [x.astype(BF16) for x in state]
        for i, (d, h) in enumerate(chains):
            c = dirs[d]["order"][step]
            r0 = c * DN_CHUNK
            v_new[i][c] = u[i][r0:r0 + DN_CHUNK] - _dot(w[i][r0:r0 + DN_CHUNK], sb[i])
            o_q[i][c] = _dot(q_dec[i][r0:r0 + DN_CHUNK], sb[i])
        for i, (d, h) in enumerate(chains):
            c = dirs[d]["order"][step]
            r = dirs[d]["last_row"][c]
            vn_full = jnp.concatenate(
                [v_new[i][c].astype(BF16) if cc == c else zeros_c for cc in range(nch)], axis=0)
            state[i] = state[i] * jnp.exp(g_cols[i][r:r + 1]) + _dot(kd_t[i], vn_full)
    for i in range(nc):
        s_ref[i] = state[i]

    for i, (d, h) in enumerate(chains):
        lo = h * DN_HEAD_DIM
        vn_all = jnp.concatenate(v_new[i], axis=0).astype(BF16)
        o = jnp.concatenate(o_q[i], axis=0) + _dot(qkd[i], vn_all)
        refs[d][5][:, lo:lo + DN_HEAD_DIM] = o.astype(BF16)


def _delta_call(dq, dk, dv, bg, bgt, seq_len):
    n = dq.shape[0]
    batch = n // seq_len
    nb = seq_len // DN_BLK
    fwd = lambda b, i: (b * nb + i, 0)
    bwd = lambda b, i: (b * nb + nb - 1 - i, 0)
    fwd_t = lambda b, i: (0, b * nb + i)
    bwd_t = lambda b, i: (0, b * nb + nb - 1 - i)
    tok = lambda m: pl.BlockSpec((DN_BLK, DN_W), m)
    in_specs = [tok(fwd), tok(fwd), tok(fwd), pl.BlockSpec((DN_BLK, LANES), fwd),
                pl.BlockSpec((4 * N_DN_HEADS, DN_BLK), fwd_t),
                tok(bwd), tok(bwd), tok(bwd), pl.BlockSpec((DN_BLK, LANES), bwd),
                pl.BlockSpec((4 * N_DN_HEADS, DN_BLK), bwd_t)]
    return pl.pallas_call(
        _delta_kernel,
        grid=(batch, nb),
        in_specs=in_specs,
        out_specs=(tok(fwd), tok(bwd)),
        out_shape=(jax.ShapeDtypeStruct((n, DN_W), BF16), jax.ShapeDtypeStruct((n, DN_W), BF16)),
        scratch_shapes=[pltpu.VMEM((DN_DIRS * N_DN_HEADS, DN_HEAD_DIM, DN_HEAD_DIM), F32)],
        compiler_params=pltpu.CompilerParams(dimension_semantics=("arbitrary", "arbitrary"),
                                             vmem_limit_bytes=VMEM_LIMIT_BYTES),
        name="delta",
    )(dq, dk, dv, bg, bgt, dq, dk, dv, bg, bgt)


def _out_kernel(x_ref, attn_ref, of_ref, ob_ref, sz_ref, gate_ref, dng_ref, wa_ref, wb_ref, wo_ref,
                g1p_ref, g2_ref, w1_ref, w2_ref, g2p_ref, y_ref):
    o = of_ref[...].astype(F32) + ob_ref[...].astype(F32)
    dng = dng_ref[...]
    dn = jnp.concatenate([_rms(o[:, h * DN_HEAD_DIM:(h + 1) * DN_HEAD_DIM], dng)
                          for h in range(N_DN_HEADS)], axis=1)
    dn = (dn * sz_ref[...].astype(F32)).astype(BF16)
    a = _dot(attn_ref[...], wa_ref[...])
    dd = _dot(dn, wb_ref[...])
    gate = gate_ref[...].astype(F32)
    merged = (gate[:, :D_MODEL] * a + gate[:, D_MODEL:] * dd).astype(BF16)
    h1 = x_ref[...] + _rms(_dot(merged, wo_ref[...]), g1p_ref[...])
    hid = _dot(_rms(h1, g2_ref[...]).astype(BF16), w1_ref[...])
    hid = jnp.square(jnp.maximum(hid, 0.0)).astype(BF16)
    y_ref[...] = h1 + _rms(_dot(hid, w2_ref[...]), g2p_ref[...])


def _out_call(x2d, attn2d, o_f, o_b, sz, gate, dng, wa, wb, wo, g1p, g2, w1, w2, g2p):
    n = x2d.shape[0]
    tm = OUT_TM
    tok = lambda w: pl.BlockSpec((tm, w), lambda i: (i, 0))
    full = lambda a: pl.BlockSpec(a.shape, lambda i: (0, 0))
    return pl.pallas_call(
        _out_kernel,
        grid=(n // tm,),
        in_specs=[tok(D_MODEL), tok(ATT_Q), tok(DN_W), tok(DN_W), tok(DN_W), tok(2 * D_MODEL),
                  full(dng), full(wa), full(wb), full(wo), full(g1p), full(g2), full(w1), full(w2), full(g2p)],
        out_specs=tok(D_MODEL),
        out_shape=jax.ShapeDtypeStruct((n, D_MODEL), F32),
        compiler_params=pltpu.CompilerParams(dimension_semantics=("arbitrary",),
                                             vmem_limit_bytes=VMEM_LIMIT_BYTES),
        name="outmlp",
    )(x2d, attn2d, o_f, o_b, sz, gate, dng, wa, wb, wo, g1p, g2, w1, w2, g2p)


def _rope_table_t(seq_len):
    pos = jnp.arange(seq_len, dtype=jnp.int32)
    row_ids = (pos // GRID_W).astype(F32)
    col_ids = (pos % GRID_W).astype(F32)
    inv_freq = ROPE_THETA ** (-jnp.arange(0, ROPE_HALF, 2, dtype=F32) / ROPE_HALF)
    ang_r = inv_freq[:, None] * row_ids[None, :]
    ang_c = inv_freq[:, None] * col_ids[None, :]
    return jnp.concatenate([jnp.cos(ang_r), jnp.sin(ang_r), jnp.cos(ang_c), jnp.sin(ang_c)], axis=0)


def _pack_w_in(w_in):
    aq, ak, av, dqkv, dz, dbeta, da, gates = jnp.split(
        w_in, [ATT_Q, ATT_Q + ATT_KV, ATT_Q + 2 * ATT_KV, C_Z, C_Z + DN_W,
               C_Z + DN_W + 2 * N_DN_HEADS, C_Z + DN_W + 4 * N_DN_HEADS], axis=-1)
    pad = jnp.zeros((D_MODEL, LANES - 4 * N_DN_HEADS), w_in.dtype)
    return jnp.concatenate([aq, ak, av, dqkv, dz, gates, dbeta, da, pad], axis=-1).astype(BF16)


def _lane_row(v):
    v = v.reshape(-1).astype(F32)
    return jnp.zeros((1, LANES), F32).at[0, 2 * N_DN_HEADS:4 * N_DN_HEADS].set(v)


def _layer(x, p):
    batch, seq_len, _ = x.shape
    n = batch * seq_len
    x2d = x.reshape(n, D_MODEL)
    rope_t = _rope_table_t(seq_len)
    qt, k, vt, dq, dk, dv, sz, gate, bg, bgt = _proj_call(
        x2d, seq_len, p["g1"], p["w_pack"], rope_t, p["qg_t"], p["kg_t"], p["conv_w"], p["alog"], p["dtb"])
    attn = _attn_call(qt, k, vt).reshape(n, ATT_Q)
    o_f, o_b = _delta_call(dq, dk, dv, bg, bgt, seq_len)
    y = _out_call(x2d, attn, o_f, o_b, sz, gate, p["dng"], p["wa"], p["wb"], p["wo"],
                  p["g1p"], p["g2"], p["w1"], p["w2"], p["g2p"])
    return y.reshape(batch, seq_len, D_MODEL)


def kernel(x_prompt, x_sample, ln1_pre_g, w_in, attn_q_norm_g, attn_k_norm_g, dn_conv_w, dn_A_log, dn_dt_bias, dn_out_norm_g, w_attn_branch, w_dn_branch, w_out, ln1_post_g, ln2_pre_g, w_ff_in, w_ff_out, ln2_post_g):
    depth = w_in.shape[0]
    outs = []
    for x in (x_prompt, x_sample):
        for l in range(depth):
            p = {
                "g1": ln1_pre_g[l].reshape(1, D_MODEL),
                "w_pack": _pack_w_in(w_in[l]),
                "qg_t": jnp.broadcast_to(attn_q_norm_g[l][:, None], (HEAD_DIM, PROJ_TM)),
                "kg_t": jnp.broadcast_to(attn_k_norm_g[l][:, None], (HEAD_DIM, PROJ_TM)),
                "conv_w": dn_conv_w[l],
                "alog": _lane_row(dn_A_log[l]),
                "dtb": _lane_row(dn_dt_bias[l]),
                "dng": dn_out_norm_g[l].reshape(1, DN_HEAD_DIM),
                "wa": w_attn_branch[l].astype(BF16),
                "wb": w_dn_branch[l].astype(BF16),
                "wo": w_out[l].astype(BF16),
                "g1p": ln1_post_g[l].reshape(1, D_MODEL),
                "g2": ln2_pre_g[l].reshape(1, D_MODEL),
                "w1": w_ff_in[l].astype(BF16),
                "w2": w_ff_out[l].astype(BF16),
                "g2p": ln2_post_g[l].reshape(1, D_MODEL),
            }
            x = _layer(x, p)
        outs.append(x)
    return tuple(outs)
```

```python
import functools
import math

import jax
import jax.numpy as jnp
from jax import lax
from jax.experimental import pallas as pl
from jax.experimental.pallas import tpu as pltpu

D_MODEL = 1024
GRID_W = 64
N_Q_HEADS = 8
N_KV_HEADS = 2
HEAD_DIM = 64
GQA_GROUP = N_Q_HEADS // N_KV_HEADS
ROPE_HALF = HEAD_DIM // 2
ROPE_FREQS = ROPE_HALF // 2
ROPE_THETA = 10000.0
N_DN_HEADS = 4
DN_HEAD_DIM = 128
DN_CHUNK = 64
D_FF = 4 * D_MODEL
EPS = 1e-6
ATT_Q = N_Q_HEADS * HEAD_DIM
ATT_KV = N_KV_HEADS * HEAD_DIM
DN_W = N_DN_HEADS * DN_HEAD_DIM

LANES = 128
SUBLANES = 8
VMEM_LIMIT_BYTES = 56 * 1024 * 1024

C_Q = 0
C_K = C_Q + ATT_Q
C_V = C_K + ATT_KV
C_D = C_V + ATT_KV
C_Z = C_D + 3 * DN_W
C_G = C_Z + DN_W
C_B = C_G + 2 * D_MODEL
PACK_W = C_B + LANES

PROJ_TM = 256
OUT_TM = 512
ATT_TQ = 256
ATT_KVT = 512
ATT_VT = 256
ATT_SLOTS = 2
DN_BLK = 4 * DN_CHUNK
DN_DIRS = 2

F32 = jnp.float32
BF16 = jnp.bfloat16
NEG_BIG = -1e30


def _rms(x, g):
    ms = jnp.mean(x * x, axis=-1, keepdims=True)
    return x * lax.rsqrt(ms + EPS) * g


def _sigmoid(x):
    return 1.0 / (1.0 + jnp.exp(-x))


def _dot(a, b):
    return jnp.dot(a, b, preferred_element_type=F32)


def _dot_nt(a, b):
    return lax.dot_general(a, b, (((1,), (1,)), ((), ())), preferred_element_type=F32)


def _norm_rope_t(xt, gain_t, rope_t):
    ms = jnp.mean(xt * xt, axis=0, keepdims=True)
    xt = xt * lax.rsqrt(ms + EPS) * gain_t
    f = ROPE_FREQS
    x1r, x2r, x1c, x2c = xt[0:f], xt[f:2 * f], xt[2 * f:3 * f], xt[3 * f:4 * f]
    cr, sr, cc, sc = rope_t[0:f], rope_t[f:2 * f], rope_t[2 * f:3 * f], rope_t[3 * f:4 * f]
    return jnp.concatenate([x1r * cr - x2r * sr, x2r * cr + x1r * sr,
                            x1c * cc - x2c * sc, x2c * cc + x1c * sc], axis=0)


def _proj_kernel(x_ref, xp_ref, xn_ref, g1_ref, w_ref, rope_ref, qg_ref, kg_ref, conv_ref,
                 alog_ref, dtb_ref,
                 qt_ref, k_ref, vt_ref, dq_ref, dk_ref, dv_ref, sz_ref, gate_ref, bg_ref, bgt_ref,
                 *, tiles_per_seq):
    tm = x_ref.shape[0]
    ti = pl.program_id(0) % tiles_per_seq
    g1 = g1_ref[...]
    xb = _rms(x_ref[...], g1).astype(BF16)

    rope_t = rope_ref[...]
    yq_t = _dot(xb, w_ref[:, C_Q:C_Q + ATT_Q]).T
    q_scale = (HEAD_DIM ** -0.5) * math.log2(math.e)
    qg = qg_ref[...]
    for h in range(N_Q_HEADS):
        qh = _norm_rope_t(yq_t[h * HEAD_DIM:(h + 1) * HEAD_DIM], qg, rope_t)
        qt_ref[0, h] = (qh * q_scale).astype(BF16)

    yk_t = _dot(xb, w_ref[:, C_K:C_K + ATT_KV]).T
    kg = kg_ref[...]
    k_t = jnp.concatenate([_norm_rope_t(yk_t[h * HEAD_DIM:(h + 1) * HEAD_DIM], kg, rope_t)
                           for h in range(N_KV_HEADS)], axis=0)
    k_ref[0] = k_t.T.astype(BF16)
    yv_t = _dot(xb, w_ref[:, C_V:C_V + ATT_KV]).T.astype(BF16)
    for c in range(tm // ATT_VT):
        vt_ref[0, c] = yv_t[:, c * ATT_VT:(c + 1) * ATT_VT]

    wd = w_ref[:, C_D:C_D + 3 * DN_W]
    y = _dot(xb, wd)
    yp = _dot(_rms(xp_ref[...], g1).astype(BF16), wd)[SUBLANES - 1:SUBLANES]
    yn = _dot(_rms(xn_ref[...], g1).astype(BF16), wd)[0:1]
    yp = jnp.where(ti == 0, 0.0, yp)
    yn = jnp.where(ti == tiles_per_seq - 1, 0.0, yn)
    row = lax.broadcasted_iota(jnp.int32, (tm, 1), 0)
    y_prev = jnp.where(row == 0, yp, pltpu.roll(y, 1, axis=0))
    y_next = jnp.where(row == tm - 1, yn, pltpu.roll(y, tm - 1, axis=0))
    cw = conv_ref[...]
    c = cw[0:1] * y_prev + cw[1:2] * y + cw[2:3] * y_next
    s = c * _sigmoid(c)
    for part, ref, scale in ((0, dq_ref, DN_HEAD_DIM ** -0.5), (1, dk_ref, 1.0)):
        outs = []
        for h in range(N_DN_HEADS):
            lo = part * DN_W + h * DN_HEAD_DIM
            xh = s[:, lo:lo + DN_HEAD_DIM]
            ss = jnp.sum(xh * xh, axis=-1, keepdims=True)
            outs.append(xh * (lax.rsqrt(ss + EPS) * scale))
        ref[...] = jnp.concatenate(outs, axis=1).astype(BF16)
    dv_ref[...] = s[:, 2 * DN_W:3 * DN_W].astype(BF16)

    z = _dot(xb, w_ref[:, C_Z:C_Z + DN_W])
    sz_ref[...] = (z * _sigmoid(z)).astype(BF16)
    gate_ref[...] = _sigmoid(_dot(xb, w_ref[:, C_G:C_G + 2 * D_MODEL])).astype(BF16)

    yb = _dot(xb, w_ref[:, C_B:C_B + LANES])
    lane = lax.broadcasted_iota(jnp.int32, (1, LANES), 1)
    t = yb + dtb_ref[...]
    softplus = jnp.maximum(t, 0.0) + jnp.log1p(jnp.exp(-jnp.abs(t)))
    bg = jnp.where(lane < 2 * N_DN_HEADS, _sigmoid(yb), -jnp.exp(alog_ref[...]) * softplus)
    bg = jnp.where(lane < 4 * N_DN_HEADS, bg, 0.0)
    bg_ref[...] = bg
    bgt_ref[...] = bg.T[0:4 * N_DN_HEADS]


def _proj_call(x2d, seq_len, g1, w_pack, rope_t, qg_t, kg_t, conv_w, alog_row, dtb_row):
    n = x2d.shape[0]
    tm = PROJ_TM
    batch = n // seq_len
    tps = seq_len // tm
    hb = tm // SUBLANES
    n8 = n // SUBLANES
    const = lambda i: (0, 0)
    tok = lambda i: (i, 0)
    in_specs = [
        pl.BlockSpec((tm, D_MODEL), tok),
        pl.BlockSpec((SUBLANES, D_MODEL), lambda i: (jnp.maximum(i * hb - 1, 0), 0)),
        pl.BlockSpec((SUBLANES, D_MODEL), lambda i: (jnp.minimum((i + 1) * hb, n8 - 1), 0)),
        pl.BlockSpec((1, D_MODEL), const),
        pl.BlockSpec((D_MODEL, PACK_W), const, pipeline_mode=pl.Buffered(1)),
        pl.BlockSpec((HEAD_DIM, tm), lambda i: (0, i % tps)),
        pl.BlockSpec((HEAD_DIM, tm), const),
        pl.BlockSpec((HEAD_DIM, tm), const),
        pl.BlockSpec((3, 3 * DN_W), const),
        pl.BlockSpec((1, LANES), const),
        pl.BlockSpec((1, LANES), const),
    ]
    out_shape = (
        jax.ShapeDtypeStruct((batch, N_Q_HEADS, HEAD_DIM, seq_len), BF16),
        jax.ShapeDtypeStruct((batch, seq_len, ATT_KV), BF16),
        jax.ShapeDtypeStruct((batch, seq_len // ATT_VT, ATT_KV, ATT_VT), BF16),
        jax.ShapeDtypeStruct((n, DN_W), BF16),
        jax.ShapeDtypeStruct((n, DN_W), BF16),
        jax.ShapeDtypeStruct((n, DN_W), BF16),
        jax.ShapeDtypeStruct((n, DN_W), BF16),
        jax.ShapeDtypeStruct((n, 2 * D_MODEL), BF16),
        jax.ShapeDtypeStruct((n, LANES), F32),
        jax.ShapeDtypeStruct((4 * N_DN_HEADS, n), F32),
    )
    kvc = tm // ATT_VT
    out_specs = (
        pl.BlockSpec((1, N_Q_HEADS, HEAD_DIM, tm), lambda i: (i // tps, 0, 0, i % tps)),
        pl.BlockSpec((1, tm, ATT_KV), lambda i: (i // tps, i % tps, 0)),
        pl.BlockSpec((1, kvc, ATT_KV, ATT_VT), lambda i: (i // tps, i % tps, 0, 0)),
        pl.BlockSpec((tm, DN_W), tok),
        pl.BlockSpec((tm, DN_W), tok),
        pl.BlockSpec((tm, DN_W), tok),
        pl.BlockSpec((tm, DN_W), tok),
        pl.BlockSpec((tm, 2 * D_MODEL), tok),
        pl.BlockSpec((tm, LANES), tok),
        pl.BlockSpec((4 * N_DN_HEADS, tm), lambda i: (0, i)),
    )
    return pl.pallas_call(
        functools.partial(_proj_kernel, tiles_per_seq=tps),
        grid=(n // tm,),
        in_specs=in_specs,
        out_specs=out_specs,
        out_shape=out_shape,
        compiler_params=pltpu.CompilerParams(dimension_semantics=("arbitrary",),
                                             vmem_limit_bytes=VMEM_LIMIT_BYTES),
        name="proj",
    )(x2d, x2d, x2d, g1, w_pack, rope_t, qg_t, kg_t, conv_w, alog_row, dtb_row)


def _attn_kernel(q_ref, k_ref, v_ref, o_ref, s_scr, acc_scr, *, n_kv_tiles):
    kvh = pl.program_id(1)
    tq = q_ref.shape[3]
    nq = GQA_GROUP * tq
    q4 = q_ref[0]
    qcat = jnp.concatenate([q4[g] for g in range(GQA_GROUP)], axis=1).astype(F32)
    zero = jnp.zeros_like(qcat)
    qpad = jnp.where(kvh == 0, jnp.concatenate([qcat, zero], axis=0),
                     jnp.concatenate([zero, qcat], axis=0)).astype(BF16)
    vt_per_tile = ATT_KVT // ATT_VT

    def scores(j):
        kt = k_ref[0, pl.ds(pl.multiple_of(j * ATT_KVT, ATT_KVT), ATT_KVT), :]
        return _dot(kt, qpad)

    def update(slot, j, m, l):
        s = s_scr[slot]
        m_new = jnp.maximum(m, jnp.max(s, axis=0, keepdims=True))
        alpha = jnp.exp2(m - m_new)
        p = jnp.exp2(s - m_new)
        l = alpha * l + jnp.sum(p, axis=0, keepdims=True)
        vt = jnp.concatenate([v_ref[0, j * vt_per_tile + c] for c in range(vt_per_tile)], axis=1)
        acc_scr[...] = alpha * acc_scr[...] + _dot(vt, p.astype(BF16))
        return m_new, l

    acc_scr[...] = jnp.zeros_like(acc_scr)
    s_scr[0] = scores(0)

    def pair(j0, m, l, produce_next):
        s_scr[1] = scores(j0 + 1)
        m, l = update(0, j0, m, l)
        if produce_next:
            s_scr[0] = scores(j0 + 2)
        return update(1, j0 + 1, m, l)

    init = (jnp.full((1, nq), NEG_BIG, F32), jnp.zeros((1, nq), F32))
    n_pairs = n_kv_tiles // ATT_SLOTS
    m, l = lax.fori_loop(0, n_pairs - 1, lambda jj, c: pair(ATT_SLOTS * jj, c[0], c[1], True), init)
    _, l = pair(ATT_SLOTS * (n_pairs - 1), m, l, False)
    out = acc_scr[...] * (1.0 / l)
    out = jnp.concatenate([out[:, g * tq:(g + 1) * tq] for g in range(GQA_GROUP)], axis=0)
    o_ref[0] = out.T.astype(BF16)


def _attn_call(qt, k, vt):
    batch, _, _, seq_len = qt.shape
    tq = ATT_TQ
    nkv = seq_len // ATT_KVT
    assert nkv % ATT_SLOTS == 0
    nq = GQA_GROUP * tq
    return pl.pallas_call(
        functools.partial(_attn_kernel, n_kv_tiles=nkv),
        grid=(batch, N_KV_HEADS, seq_len // tq),
        in_specs=[
            pl.BlockSpec((1, GQA_GROUP, HEAD_DIM, tq), lambda b, h, i: (b, h, 0, i)),
            pl.BlockSpec((1, seq_len, ATT_KV), lambda b, h, i: (b, 0, 0)),
            pl.BlockSpec((1, seq_len // ATT_VT, HEAD_DIM, ATT_VT), lambda b, h, i: (b, 0, h, 0)),
        ],
        out_specs=pl.BlockSpec((1, tq, GQA_GROUP * HEAD_DIM), lambda b, h, i: (b, i, h)),
        out_shape=jax.ShapeDtypeStruct((batch, seq_len, ATT_Q), BF16),
        scratch_shapes=[pltpu.VMEM((ATT_SLOTS, ATT_KVT, nq), F32), pltpu.VMEM((HEAD_DIM, nq), F32)],
        compiler_params=pltpu.CompilerParams(
            dimension_semantics=("arbitrary", "arbitrary", "arbitrary"),
            vmem_limit_bytes=VMEM_LIMIT_BYTES),
        name="attn",
    )(qt, k, vt)


def _split3(x):
    p1 = x.astype(BF16)
    r1 = x - p1.astype(F32)
    p2 = r1.astype(BF16)
    p3 = (r1 - p2.astype(F32)).astype(BF16)
    return p1, p2, p3


def _delta_kernel(qf_ref, kf_ref, vf_ref, bgf_ref, bgtf_ref, qb_ref, kb_ref, vb_ref, bgb_ref, bgtb_ref,
                  of_ref, ob_ref, s_ref):
    @pl.when(pl.program_id(1) == 0)
    def _():
        s_ref[...] = jnp.zeros_like(s_ref)

    blk = DN_BLK
    nch = blk // DN_CHUNK
    ri = lax.broadcasted_iota(jnp.int32, (blk, blk), 0)
    ci = lax.broadcasted_iota(jnp.int32, (blk, blk), 1)
    same = (ri // DN_CHUNK) == (ci // DN_CHUNK)
    lower_incl, lower_strict = same & (ri >= ci), same & (ri > ci)
    upper_incl, upper_strict = same & (ri <= ci), same & (ri < ci)
    eye = (ri == ci).astype(F32)
    refs = ((qf_ref, kf_ref, vf_ref, bgf_ref, bgtf_ref, of_ref),
            (qb_ref, kb_ref, vb_ref, bgb_ref, bgtb_ref, ob_ref))

    dirs = []
    for d in range(DN_DIRS):
        reverse = d == 1
        incl, strict, incl_t = ((upper_incl, upper_strict, lower_incl) if reverse
                                else (lower_incl, lower_strict, upper_incl))
        bg = refs[d][3][...]
        bgt = refs[d][4][...]
        m_col = jnp.where(incl, 1.0, 0.0).astype(BF16)
        m_row = jnp.where(incl_t, 1.0, 0.0).astype(BF16)
        gc = sum(_dot(m_col, piece) for piece in _split3(bg))
        gct = sum(_dot(piece, m_row) for piece in _split3(bgt))
        last_row = [(c * DN_CHUNK if reverse else (c + 1) * DN_CHUNK - 1) for c in range(nch)]
        order = list(range(nch - 1, -1, -1)) if reverse else list(range(nch))
        dirs.append(dict(incl=incl, strict=strict, bg=bg, gc=gc, gct=gct, last_row=last_row, order=order))

    chains = [(d, h) for d in range(DN_DIRS) for h in range(N_DN_HEADS)]
    nc = len(chains)

    qh, kh, vh, kk, qk = [], [], [], [], []
    for d, h in chains:
        lo = h * DN_HEAD_DIM
        qh.append(refs[d][0][:, lo:lo + DN_HEAD_DIM])
        kh.append(refs[d][1][:, lo:lo + DN_HEAD_DIM])
        vh.append(refs[d][2][:, lo:lo + DN_HEAD_DIM])
    for i in range(nc):
        kk.append(_dot_nt(kh[i], kh[i]))
        qk.append(_dot_nt(qh[i], kh[i]))

    low, qkd, rhs, q_dec, kd_t, g_cols = [], [], [], [], [], []
    for i, (d, h) in enumerate(chains):
        dd = dirs[d]
        cb = d * N_DN_HEADS + h
        cg = 2 * N_DN_HEADS + cb
        b_col = dd["bg"][:, cb:cb + 1]
        g_col = dd["gc"][:, cg:cg + 1]
        g_row = dd["gct"][cg:cg + 1, :]
        decay = jnp.exp(jnp.where(dd["incl"], g_col - g_row, NEG_BIG))
        low.append(jnp.where(dd["strict"], b_col * kk[i] * decay, 0.0))
        qkd.append((qk[i] * decay).astype(BF16))
        e_col = jnp.exp(g_col)
        kf = kh[i].astype(F32)
        rhs.append(jnp.concatenate([vh[i].astype(F32) * b_col, kf * (b_col * e_col)], axis=1).astype(BF16))
        q_dec.append((qh[i].astype(F32) * e_col).astype(BF16))
        gl_col = jnp.concatenate(
            [jnp.broadcast_to(g_col[r:r + 1], (DN_CHUNK, 1)) for r in dd["last_row"]], axis=0)
        kd_t.append((kf * jnp.exp(gl_col - g_col)).T.astype(BF16))
        g_cols.append(g_col)

    def compact(x):
        return sum(x[c * DN_CHUNK:(c + 1) * DN_CHUNK] for c in range(nch))

    def expand(xc):
        return jnp.where(same, jnp.concatenate([xc] * nch, axis=0), 0.0).astype(BF16)

    eye_c = compact(eye)
    low_c = [compact(x) for x in low]
    xc = [_dot(low_c[i].astype(BF16), low[i].astype(BF16)) for i in range(nc)]
    rc = [eye_c - low_c[i] for i in range(nc)]
    n_sq = int(math.log2(DN_CHUNK)) - 1
    for it in range(n_sq):
        xbd = [expand(x) for x in xc]
        if it < n_sq - 1:
            both = [_dot(jnp.concatenate([rc[i], xc[i]], axis=0).astype(BF16), xbd[i]) for i in range(nc)]
            rc = [rc[i] + both[i][:DN_CHUNK] for i in range(nc)]
            xc = [both[i][DN_CHUNK:] for i in range(nc)]
        else:
            rc = [rc[i] + _dot(rc[i].astype(BF16), xbd[i]) for i in range(nc)]

    uw = [_dot(expand(rc[i]), rhs[i]) for i in range(nc)]
    u = [x[:, :DN_HEAD_DIM] for x in uw]
    w = [x[:, DN_HEAD_DIM:].astype(BF16) for x in uw]

    state = [s_ref[i] for i in range(nc)]
    v_new = [[None] * nch for _ in range(nc)]
    o_q = [[None] * nch for _ in range(nc)]
    zeros_c = jnp.zeros((DN_CHUNK, DN_HEAD_DIM), BF16)
    for step in range(nch):
        sb = [x.astype(BF16) for x in state]
        for i, (d, h) in enumerate(chains):
            c = dirs[d]["order"][step]
            r0 = c * DN_CHUNK
            wq = jnp.concatenate([w[i][r0:r0 + DN_CHUNK], q_dec[i][r0:r0 + DN_CHUNK]], axis=0)
            res = _dot(wq, sb[i])
            v_new[i][c] = u[i][r0:r0 + DN_CHUNK] - res[:DN_CHUNK]
            o_q[i][c] = res[DN_CHUNK:]
        for i, (d, h) in enumerate(chains):
            c = dirs[d]["order"][step]
            r = dirs[d]["last_row"][c]
            vn_full = jnp.concatenate(
                [v_new[i][c].astype(BF16) if cc == c else zeros_c for cc in range(nch)], axis=0)
            state[i] = state[i] * jnp.exp(g_cols[i][r:r + 1]) + _dot(kd_t[i], vn_full)
    for i in range(nc):
        s_ref[i] = state[i]

    for i, (d, h) in enumerate(chains):
        lo = h * DN_HEAD_DIM
        vn_all = jnp.concatenate(v_new[i], axis=0).astype(BF16)
        o = jnp.concatenate(o_q[i], axis=0) + _dot(qkd[i], vn_all)
        refs[d][5][:, lo:lo + DN_HEAD_DIM] = o.astype(BF16)


def _delta_call(dq, dk, dv, bg, bgt, seq_len):
    n = dq.shape[0]
    batch = n // seq_len
    nb = seq_len // DN_BLK
    fwd = lambda b, i: (b * nb + i, 0)
    bwd = lambda b, i: (b * nb + nb - 1 - i, 0)
    fwd_t = lambda b, i: (0, b * nb + i)
    bwd_t = lambda b, i: (0, b * nb + nb - 1 - i)
    tok = lambda m: pl.BlockSpec((DN_BLK, DN_W), m)
    in_specs = [tok(fwd), tok(fwd), tok(fwd), pl.BlockSpec((DN_BLK, LANES), fwd),
                pl.BlockSpec((4 * N_DN_HEADS, DN_BLK), fwd_t),
                tok(bwd), tok(bwd), tok(bwd), pl.BlockSpec((DN_BLK, LANES), bwd),
                pl.BlockSpec((4 * N_DN_HEADS, DN_BLK), bwd_t)]
    return pl.pallas_call(
        _delta_kernel,
        grid=(batch, nb),
        in_specs=in_specs,
        out_specs=(tok(fwd), tok(bwd)),
        out_shape=(jax.ShapeDtypeStruct((n, DN_W), BF16), jax.ShapeDtypeStruct((n, DN_W), BF16)),
        scratch_shapes=[pltpu.VMEM((DN_DIRS * N_DN_HEADS, DN_HEAD_DIM, DN_HEAD_DIM), F32)],
        compiler_params=pltpu.CompilerParams(dimension_semantics=("arbitrary", "arbitrary"),
                                             vmem_limit_bytes=VMEM_LIMIT_BYTES),
        name="delta",
    )(dq, dk, dv, bg, bgt, dq, dk, dv, bg, bgt)


def _out_kernel(x_ref, attn_ref, of_ref, ob_ref, sz_ref, gate_ref, dng_ref, wa_ref, wb_ref, wo_ref,
                g1p_ref, g2_ref, w1_ref, w2_ref, g2p_ref, y_ref):
    o = of_ref[...].astype(F32) + ob_ref[...].astype(F32)
    dng = dng_ref[...]
    dn = jnp.concatenate([_rms(o[:, h * DN_HEAD_DIM:(h + 1) * DN_HEAD_DIM], dng)
                          for h in range(N_DN_HEADS)], axis=1)
    dn = (dn * sz_ref[...].astype(F32)).astype(BF16)
    a = _dot(attn_ref[...], wa_ref[...])
    dd = _dot(dn, wb_ref[...])
    gate = gate_ref[...].astype(F32)
    merged = (gate[:, :D_MODEL] * a + gate[:, D_MODEL:] * dd).astype(BF16)
    h1 = x_ref[...] + _rms(_dot(merged, wo_ref[...]), g1p_ref[...])
    hid = _dot(_rms(h1, g2_ref[...]).astype(BF16), w1_ref[...])
    hid = jnp.square(jnp.maximum(hid, 0.0)).astype(BF16)
    y_ref[...] = h1 + _rms(_dot(hid, w2_ref[...]), g2p_ref[...])


def _out_call(x2d, attn2d, o_f, o_b, sz, gate, dng, wa, wb, wo, g1p, g2, w1, w2, g2p):
    n = x2d.shape[0]
    tm = OUT_TM
    tok = lambda w: pl.BlockSpec((tm, w), lambda i: (i, 0))
    full = lambda a: pl.BlockSpec(a.shape, lambda i: (0, 0), pipeline_mode=pl.Buffered(1))
    return pl.pallas_call(
        _out_kernel,
        grid=(n // tm,),
        in_specs=[tok(D_MODEL), tok(ATT_Q), tok(DN_W), tok(DN_W), tok(DN_W), tok(2 * D_MODEL),
                  full(dng), full(wa), full(wb), full(wo), full(g1p), full(g2), full(w1), full(w2), full(g2p)],
        out_specs=tok(D_MODEL),
        out_shape=jax.ShapeDtypeStruct((n, D_MODEL), F32),
        compiler_params=pltpu.CompilerParams(dimension_semantics=("arbitrary",),
                                             vmem_limit_bytes=VMEM_LIMIT_BYTES),
        name="outmlp",
    )(x2d, attn2d, o_f, o_b, sz, gate, dng, wa, wb, wo, g1p, g2, w1, w2, g2p)


def _rope_table_t(seq_len):
    pos = jnp.arange(seq_len, dtype=jnp.int32)
    row_ids = (pos // GRID_W).astype(F32)
    col_ids = (pos % GRID_W).astype(F32)
    inv_freq = ROPE_THETA ** (-jnp.arange(0, ROPE_HALF, 2, dtype=F32) / ROPE_HALF)
    ang_r = inv_freq[:, None] * row_ids[None, :]
    ang_c = inv_freq[:, None] * col_ids[None, :]
    return jnp.concatenate([jnp.cos(ang_r), jnp.sin(ang_r), jnp.cos(ang_c), jnp.sin(ang_c)], axis=0)


def _pack_w_in(w_in):
    aq, ak, av, dqkv, dz, dbeta, da, gates = jnp.split(
        w_in, [ATT_Q, ATT_Q + ATT_KV, ATT_Q + 2 * ATT_KV, C_Z, C_Z + DN_W,
               C_Z + DN_W + 2 * N_DN_HEADS, C_Z + DN_W + 4 * N_DN_HEADS], axis=-1)
    pad = jnp.zeros((D_MODEL, LANES - 4 * N_DN_HEADS), w_in.dtype)
    return jnp.concatenate([aq, ak, av, dqkv, dz, gates, dbeta, da, pad], axis=-1).astype(BF16)


def _lane_row(v):
    v = v.reshape(-1).astype(F32)
    return jnp.zeros((1, LANES), F32).at[0, 2 * N_DN_HEADS:4 * N_DN_HEADS].set(v)


def _layer(x, p):
    batch, seq_len, _ = x.shape
    n = batch * seq_len
    x2d = x.reshape(n, D_MODEL)
    rope_t = _rope_table_t(seq_len)
    qt, k, vt, dq, dk, dv, sz, gate, bg, bgt = _proj_call(
        x2d, seq_len, p["g1"], p["w_pack"], rope_t, p["qg_t"], p["kg_t"], p["conv_w"], p["alog"], p["dtb"])
    attn = _attn_call(qt, k, vt).reshape(n, ATT_Q)
    o_f, o_b = _delta_call(dq, dk, dv, bg, bgt, seq_len)
    y = _out_call(x2d, attn, o_f, o_b, sz, gate, p["dng"], p["wa"], p["wb"], p["wo"],
                  p["g1p"], p["g2"], p["w1"], p["w2"], p["g2p"])
    return y.reshape(batch, seq_len, D_MODEL)


def kernel(x_prompt, x_sample, ln1_pre_g, w_in, attn_q_norm_g, attn_k_norm_g, dn_conv_w, dn_A_log, dn_dt_bias, dn_out_norm_g, w_attn_branch, w_dn_branch, w_out, ln1_post_g, ln2_pre_g, w_ff_in, w_ff_out, ln2_post_g):
    depth = w_in.shape[0]
    outs = []
    for x in (x_prompt, x_sample):
        for l in range(depth):
            p = {
                "g1": ln1_pre_g[l].reshape(1, D_MODEL),
                "w_pack": _pack_w_in(w_in[l]),
                "qg_t": jnp.broadcast_to(attn_q_norm_g[l][:, None], (HEAD_DIM, PROJ_TM)),
                "kg_t": jnp.broadcast_to(attn_k_norm_g[l][:, None], (HEAD_DIM, PROJ_TM)),
                "conv_w": dn_conv_w[l],
                "alog": _lane_row(dn_A_log[l]),
                "dtb": _lane_row(dn_dt_bias[l]),
                "dng": dn_out_norm_g[l].reshape(1, DN_HEAD_DIM),
                "wa": w_attn_branch[l].astype(BF16),
                "wb": w_dn_branch[l].astype(BF16),
                "wo": w_out[l].astype(BF16),
                "g1p": ln1_post_g[l].reshape(1, D_MODEL),
                "g2": ln2_pre_g[l].reshape(1, D_MODEL),
                "w1": w_ff_in[l].astype(BF16),
                "w2": w_ff_out[l].astype(BF16),
                "g2p": ln2_post_g[l].reshape(1, D_MODEL),
            }
            x = _layer(x, p)
        outs.append(x)
    return tuple(outs)
```

```python
import functools
import math

import jax
import jax.numpy as jnp
from jax import lax
from jax.experimental import pallas as pl
from jax.experimental.pallas import tpu as pltpu

D_MODEL = 1024
GRID_W = 64
N_Q_HEADS = 8
N_KV_HEADS = 2
HEAD_DIM = 64
GQA_GROUP = N_Q_HEADS // N_KV_HEADS
ROPE_HALF = HEAD_DIM // 2
ROPE_FREQS = ROPE_HALF // 2
ROPE_THETA = 10000.0
N_DN_HEADS = 4
DN_HEAD_DIM = 128
DN_CHUNK = 64
D_FF = 4 * D_MODEL
EPS = 1e-6
ATT_Q = N_Q_HEADS * HEAD_DIM
ATT_KV = N_KV_HEADS * HEAD_DIM
DN_W = N_DN_HEADS * DN_HEAD_DIM

LANES = 128
SUBLANES = 8
VMEM_LIMIT_BYTES = 56 * 1024 * 1024

C_Q = 0
C_K = C_Q + ATT_Q
C_V = C_K + ATT_KV
C_D = C_V + ATT_KV
C_Z = C_D + 3 * DN_W
C_G = C_Z + DN_W
C_B = C_G + 2 * D_MODEL
PACK_W = C_B + LANES

PROJ_TM = 512
OUT_TM = 512
ATT_TQ = 512
ATT_KVT = 512
ATT_VT = 256
ATT_SLOTS = 2
DN_BLK = 4 * DN_CHUNK
DN_DIRS = 2

F32 = jnp.float32
BF16 = jnp.bfloat16
NEG_BIG = -1e30


def _rms(x, g):
    ms = jnp.mean(x * x, axis=-1, keepdims=True)
    return x * lax.rsqrt(ms + EPS) * g


def _sigmoid(x):
    return 1.0 / (1.0 + jnp.exp(-x))


def _dot(a, b):
    return jnp.dot(a, b, preferred_element_type=F32)


def _dot_nt(a, b):
    return lax.dot_general(a, b, (((1,), (1,)), ((), ())), preferred_element_type=F32)


def _norm_rope_t(xt, gain_t, rope_t):
    ms = jnp.mean(xt * xt, axis=0, keepdims=True)
    xt = xt * lax.rsqrt(ms + EPS) * gain_t
    f = ROPE_FREQS
    x1r, x2r, x1c, x2c = xt[0:f], xt[f:2 * f], xt[2 * f:3 * f], xt[3 * f:4 * f]
    cr, sr, cc, sc = rope_t[0:f], rope_t[f:2 * f], rope_t[2 * f:3 * f], rope_t[3 * f:4 * f]
    return jnp.concatenate([x1r * cr - x2r * sr, x2r * cr + x1r * sr,
                            x1c * cc - x2c * sc, x2c * cc + x1c * sc], axis=0)


def _proj_kernel(x_ref, xp_ref, xn_ref, g1_ref, w_ref, rope_ref, qg_ref, kg_ref, conv_ref,
                 alog_ref, dtb_ref,
                 qt_ref, k_ref, vt_ref, dq_ref, dk_ref, dv_ref, sz_ref, gate_ref, bg_ref, bgt_ref,
                 *, tiles_per_seq):
    tm = x_ref.shape[0]
    ti = pl.program_id(0) % tiles_per_seq
    g1 = g1_ref[...]
    xb = _rms(x_ref[...], g1).astype(BF16)

    rope_t = rope_ref[...]
    yq_t = _dot(xb, w_ref[:, C_Q:C_Q + ATT_Q]).T
    q_scale = (HEAD_DIM ** -0.5) * math.log2(math.e)
    qg = qg_ref[...]
    for h in range(N_Q_HEADS):
        qh = _norm_rope_t(yq_t[h * HEAD_DIM:(h + 1) * HEAD_DIM], qg, rope_t)
        qt_ref[0, h] = (qh * q_scale).astype(BF16)

    yk_t = _dot(xb, w_ref[:, C_K:C_K + ATT_KV]).T
    kg = kg_ref[...]
    k_t = jnp.concatenate([_norm_rope_t(yk_t[h * HEAD_DIM:(h + 1) * HEAD_DIM], kg, rope_t)
                           for h in range(N_KV_HEADS)], axis=0)
    k_ref[0] = k_t.T.astype(BF16)
    yv_t = _dot(xb, w_ref[:, C_V:C_V + ATT_KV]).T.astype(BF16)
    for c in range(tm // ATT_VT):
        vt_ref[0, c] = yv_t[:, c * ATT_VT:(c + 1) * ATT_VT]

    halo = _rms(jnp.concatenate([xp_ref[...], xn_ref[...]], axis=0), g1).astype(BF16)
    y_ext = _dot(jnp.concatenate([xb, halo], axis=0), w_ref[:, C_D:C_D + 3 * DN_W])
    y = y_ext[0:tm]
    yp = y_ext[tm + SUBLANES - 1:tm + SUBLANES]
    yn = y_ext[tm + SUBLANES:tm + SUBLANES + 1]
    yp = jnp.where(ti == 0, 0.0, yp)
    yn = jnp.where(ti == tiles_per_seq - 1, 0.0, yn)
    row = lax.broadcasted_iota(jnp.int32, (tm, 1), 0)
    y_prev = jnp.where(row == 0, yp, pltpu.roll(y, 1, axis=0))
    y_next = jnp.where(row == tm - 1, yn, pltpu.roll(y, tm - 1, axis=0))
    cw = conv_ref[...]
    c = cw[0:1] * y_prev + cw[1:2] * y + cw[2:3] * y_next
    s = c * _sigmoid(c)
    for part, ref, scale in ((0, dq_ref, DN_HEAD_DIM ** -0.5), (1, dk_ref, 1.0)):
        outs = []
        for h in range(N_DN_HEADS):
            lo = part * DN_W + h * DN_HEAD_DIM
            xh = s[:, lo:lo + DN_HEAD_DIM]
            ss = jnp.sum(xh * xh, axis=-1, keepdims=True)
            outs.append(xh * (lax.rsqrt(ss + EPS) * scale))
        ref[...] = jnp.concatenate(outs, axis=1).astype(BF16)
    dv_ref[...] = s[:, 2 * DN_W:3 * DN_W].astype(BF16)

    z = _dot(xb, w_ref[:, C_Z:C_Z + DN_W])
    sz_ref[...] = (z * _sigmoid(z)).astype(BF16)
    gate_ref[...] = _sigmoid(_dot(xb, w_ref[:, C_G:C_G + 2 * D_MODEL])).astype(BF16)

    yb = _dot(xb, w_ref[:, C_B:C_B + LANES])
    lane = lax.broadcasted_iota(jnp.int32, (1, LANES), 1)
    t = yb + dtb_ref[...]
    softplus = jnp.maximum(t, 0.0) + jnp.log1p(jnp.exp(-jnp.abs(t)))
    bg = jnp.where(lane < 2 * N_DN_HEADS, _sigmoid(yb), -jnp.exp(alog_ref[...]) * softplus)
    bg = jnp.where(lane < 4 * N_DN_HEADS, bg, 0.0)
    bg_ref[...] = bg
    bgt_ref[...] = bg.T[0:4 * N_DN_HEADS]


def _proj_call(x2d, seq_len, g1, w_pack, rope_t, qg_t, kg_t, conv_w, alog_row, dtb_row):
    n = x2d.shape[0]
    tm = PROJ_TM
    batch = n // seq_len
    tps = seq_len // tm
    hb = tm // SUBLANES
    n8 = n // SUBLANES
    const = lambda i: (0, 0)
    tok = lambda i: (i, 0)
    in_specs = [
        pl.BlockSpec((tm, D_MODEL), tok),
        pl.BlockSpec((SUBLANES, D_MODEL), lambda i: (jnp.maximum(i * hb - 1, 0), 0)),
        pl.BlockSpec((SUBLANES, D_MODEL), lambda i: (jnp.minimum((i + 1) * hb, n8 - 1), 0)),
        pl.BlockSpec((1, D_MODEL), const),
        pl.BlockSpec((D_MODEL, PACK_W), const, pipeline_mode=pl.Buffered(1)),
        pl.BlockSpec((HEAD_DIM, tm), lambda i: (0, i % tps)),
        pl.BlockSpec((HEAD_DIM, tm), const),
        pl.BlockSpec((HEAD_DIM, tm), const),
        pl.BlockSpec((3, 3 * DN_W), const),
        pl.BlockSpec((1, LANES), const),
        pl.BlockSpec((1, LANES), const),
    ]
    out_shape = (
        jax.ShapeDtypeStruct((batch, N_Q_HEADS, HEAD_DIM, seq_len), BF16),
        jax.ShapeDtypeStruct((batch, seq_len, ATT_KV), BF16),
        jax.ShapeDtypeStruct((batch, seq_len // ATT_VT, ATT_KV, ATT_VT), BF16),
        jax.ShapeDtypeStruct((n, DN_W), BF16),
        jax.ShapeDtypeStruct((n, DN_W), BF16),
        jax.ShapeDtypeStruct((n, DN_W), BF16),
        jax.ShapeDtypeStruct((n, DN_W), BF16),
        jax.ShapeDtypeStruct((n, 2 * D_MODEL), BF16),
        jax.ShapeDtypeStruct((n, LANES), F32),
        jax.ShapeDtypeStruct((4 * N_DN_HEADS, n), F32),
    )
    kvc = tm // ATT_VT
    out_specs = (
        pl.BlockSpec((1, N_Q_HEADS, HEAD_DIM, tm), lambda i: (i // tps, 0, 0, i % tps)),
        pl.BlockSpec((1, tm, ATT_KV), lambda i: (i // tps, i % tps, 0)),
        pl.BlockSpec((1, kvc, ATT_KV, ATT_VT), lambda i: (i // tps, i % tps, 0, 0)),
        pl.BlockSpec((tm, DN_W), tok),
        pl.BlockSpec((tm, DN_W), tok),
        pl.BlockSpec((tm, DN_W), tok),
        pl.BlockSpec((tm, DN_W), tok),
        pl.BlockSpec((tm, 2 * D_MODEL), tok),
        pl.BlockSpec((tm, LANES), tok),
        pl.BlockSpec((4 * N_DN_HEADS, tm), lambda i: (0, i)),
    )
    return pl.pallas_call(
        functools.partial(_proj_kernel, tiles_per_seq=tps),
        grid=(n // tm,),
        in_specs=in_specs,
        out_specs=out_specs,
        out_shape=out_shape,
        compiler_params=pltpu.CompilerParams(dimension_semantics=("arbitrary",),
                                             vmem_limit_bytes=VMEM_LIMIT_BYTES),
        name="proj",
    )(x2d, x2d, x2d, g1, w_pack, rope_t, qg_t, kg_t, conv_w, alog_row, dtb_row)


def _attn_kernel(q_ref, k_ref, v_ref, o_ref, s_scr, acc_scr, *, n_kv_tiles):
    kvh = pl.program_id(1)
    tq = q_ref.shape[3]
    nq = GQA_GROUP * tq
    q4 = q_ref[0]
    qcat = jnp.concatenate([q4[g] for g in range(GQA_GROUP)], axis=1).astype(F32)
    zero = jnp.zeros_like(qcat)
    qpad = jnp.where(kvh == 0, jnp.concatenate([qcat, zero], axis=0),
                     jnp.concatenate([zero, qcat], axis=0)).astype(BF16)
    vt_per_tile = ATT_KVT // ATT_VT

    def scores(j):
        kt = k_ref[0, pl.ds(pl.multiple_of(j * ATT_KVT, ATT_KVT), ATT_KVT), :]
        return _dot(kt, qpad)

    def update(slot, j, m, l):
        s = s_scr[slot]
        m_new = jnp.maximum(m, jnp.max(s, axis=0, keepdims=True))
        alpha = jnp.exp2(m - m_new)
        p = jnp.exp2(s - m_new)
        l = alpha * l + jnp.sum(p, axis=0, keepdims=True)
        vt = jnp.concatenate([v_ref[0, j * vt_per_tile + c] for c in range(vt_per_tile)], axis=1)
        acc_scr[...] = alpha * acc_scr[...] + _dot(vt, p.astype(BF16))
        return m_new, l

    acc_scr[...] = jnp.zeros_like(acc_scr)
    s_scr[0] = scores(0)

    def pair(j0, m, l, produce_next):
        s_scr[1] = scores(j0 + 1)
        m, l = update(0, j0, m, l)
        if produce_next:
            s_scr[0] = scores(j0 + 2)
        return update(1, j0 + 1, m, l)

    init = (jnp.full((1, nq), NEG_BIG, F32), jnp.zeros((1, nq), F32))
    n_pairs = n_kv_tiles // ATT_SLOTS
    m, l = lax.fori_loop(0, n_pairs - 1, lambda jj, c: pair(ATT_SLOTS * jj, c[0], c[1], True), init)
    _, l = pair(ATT_SLOTS * (n_pairs - 1), m, l, False)
    out = acc_scr[...] * (1.0 / l)
    out = jnp.concatenate([out[:, g * tq:(g + 1) * tq] for g in range(GQA_GROUP)], axis=0)
    o_ref[0] = out.T.astype(BF16)


def _attn_call(qt, k, vt):
    batch, _, _, seq_len = qt.shape
    tq = ATT_TQ
    nkv = seq_len // ATT_KVT
    assert nkv % ATT_SLOTS == 0
    nq = GQA_GROUP * tq
    return pl.pallas_call(
        functools.partial(_attn_kernel, n_kv_tiles=nkv),
        grid=(batch, N_KV_HEADS, seq_len // tq),
        in_specs=[
            pl.BlockSpec((1, GQA_GROUP, HEAD_DIM, tq), lambda b, h, i: (b, h, 0, i)),
            pl.BlockSpec((1, seq_len, ATT_KV), lambda b, h, i: (b, 0, 0)),
            pl.BlockSpec((1, seq_len // ATT_VT, HEAD_DIM, ATT_VT), lambda b, h, i: (b, 0, h, 0)),
        ],
        out_specs=pl.BlockSpec((1, tq, GQA_GROUP * HEAD_DIM), lambda b, h, i: (b, i, h)),
        out_shape=jax.ShapeDtypeStruct((batch, seq_len, ATT_Q), BF16),
        scratch_shapes=[pltpu.VMEM((ATT_SLOTS, ATT_KVT, nq), F32), pltpu.VMEM((HEAD_DIM, nq), F32)],
        compiler_params=pltpu.CompilerParams(
            dimension_semantics=("arbitrary", "arbitrary", "arbitrary"),
            vmem_limit_bytes=VMEM_LIMIT_BYTES),
        name="attn",
    )(qt, k, vt)


def _split3(x):
    p1 = x.astype(BF16)
    r1 = x - p1.astype(F32)
    p2 = r1.astype(BF16)
    p3 = (r1 - p2.astype(F32)).astype(BF16)
    return p1, p2, p3


def _delta_kernel(qf_ref, kf_ref, vf_ref, bgf_ref, bgtf_ref, qb_ref, kb_ref, vb_ref, bgb_ref, bgtb_ref,
                  of_ref, ob_ref, s_ref):
    @pl.when(pl.program_id(1) == 0)
    def _():
        s_ref[...] = jnp.zeros_like(s_ref)

    blk = DN_BLK
    nch = blk // DN_CHUNK
    ri = lax.broadcasted_iota(jnp.int32, (blk, blk), 0)
    ci = lax.broadcasted_iota(jnp.int32, (blk, blk), 1)
    same = (ri // DN_CHUNK) == (ci // DN_CHUNK)
    lower_incl, lower_strict = same & (ri >= ci), same & (ri > ci)
    upper_incl, upper_strict = same & (ri <= ci), same & (ri < ci)
    eye = (ri == ci).astype(F32)
    refs = ((qf_ref, kf_ref, vf_ref, bgf_ref, bgtf_ref, of_ref),
            (qb_ref, kb_ref, vb_ref, bgb_ref, bgtb_ref, ob_ref))

    dirs = []
    for d in range(DN_DIRS):
        reverse = d == 1
        incl, strict, incl_t = ((upper_incl, upper_strict, lower_incl) if reverse
                                else (lower_incl, lower_strict, upper_incl))
        bg = refs[d][3][...]
        bgt = refs[d][4][...]
        m_col = jnp.where(incl, 1.0, 0.0).astype(BF16)
        m_row = jnp.where(incl_t, 1.0, 0.0).astype(BF16)
        gc = sum(_dot(m_col, piece) for piece in _split3(bg))
        gct = sum(_dot(piece, m_row) for piece in _split3(bgt))
        last_row = [(c * DN_CHUNK if reverse else (c + 1) * DN_CHUNK - 1) for c in range(nch)]
        order = list(range(nch - 1, -1, -1)) if reverse else list(range(nch))
        dirs.append(dict(incl=incl, strict=strict, bg=bg, gc=gc, gct=gct, last_row=last_row, order=order))

    chains = [(d, h) for d in range(DN_DIRS) for h in range(N_DN_HEADS)]
    nc = len(chains)

    qh, kh, vh, kk, qk = [], [], [], [], []
    for d, h in chains:
        lo = h * DN_HEAD_DIM
        qh.append(refs[d][0][:, lo:lo + DN_HEAD_DIM])
        kh.append(refs[d][1][:, lo:lo + DN_HEAD_DIM])
        vh.append(refs[d][2][:, lo:lo + DN_HEAD_DIM])
    for i in range(nc):
        kk.append(_dot_nt(kh[i], kh[i]))
        qk.append(_dot_nt(qh[i], kh[i]))

    low, qkd, rhs, q_dec, kd_t, g_cols = [], [], [], [], [], []
    for i, (d, h) in enumerate(chains):
        dd = dirs[d]
        cb = d * N_DN_HEADS + h
        cg = 2 * N_DN_HEADS + cb
        b_col = dd["bg"][:, cb:cb + 1]
        g_col = dd["gc"][:, cg:cg + 1]
        g_row = dd["gct"][cg:cg + 1, :]
        decay = jnp.exp(jnp.where(dd["incl"], g_col - g_row, NEG_BIG))
        low.append(jnp.where(dd["strict"], b_col * kk[i] * decay, 0.0))
        qkd.append((qk[i] * decay).astype(BF16))
        e_col = jnp.exp(g_col)
        kf = kh[i].astype(F32)
        rhs.append(jnp.concatenate([vh[i].astype(F32) * b_col, kf * (b_col * e_col)], axis=1).astype(BF16))
        q_dec.append((qh[i].astype(F32) * e_col).astype(BF16))
        gl_col = jnp.concatenate(
            [jnp.broadcast_to(g_col[r:r + 1], (DN_CHUNK, 1)) for r in dd["last_row"]], axis=0)
        kd_t.append((kf * jnp.exp(gl_col - g_col)).T.astype(BF16))
        g_cols.append(g_col)

    def compact(x):
        return sum(x[c * DN_CHUNK:(c + 1) * DN_CHUNK] for c in range(nch))

    def expand(xc):
        return jnp.where(same, jnp.concatenate([xc] * nch, axis=0), 0.0).astype(BF16)

    eye_c = compact(eye)
    low_c = [compact(x) for x in low]
    xc = [_dot(low_c[i].astype(BF16), low[i].astype(BF16)) for i in range(nc)]
    rc = [eye_c - low_c[i] for i in range(nc)]
    n_sq = int(math.log2(DN_CHUNK)) - 1
    for it in range(n_sq):
        xbd = [expand(x) for x in xc]
        if it < n_sq - 1:
            both = [_dot(jnp.concatenate([rc[i], xc[i]], axis=0).astype(BF16), xbd[i]) for i in range(nc)]
            rc = [rc[i] + both[i][:DN_CHUNK] for i in range(nc)]
            xc = [both[i][DN_CHUNK:] for i in range(nc)]
        else:
            rc = [rc[i] + _dot(rc[i].astype(BF16), xbd[i]) for i in range(nc)]

    uw = [_dot(expand(rc[i]), rhs[i]) for i in range(nc)]
    u = [x[:, :DN_HEAD_DIM] for x in uw]
    w = [x[:, DN_HEAD_DIM:].astype(BF16) for x in uw]

    state = [s_ref[i] for i in range(nc)]
    v_new = [[None] * nch for _ in range(nc)]
    o_q = [[None] * nch for _ in range(nc)]
    zeros_c = jnp.zeros((DN_CHUNK, DN_HEAD_DIM), BF16)
    for step in range(nch):
        sb = [x.astype(BF16) for x in state]
        for i, (d, h) in enumerate(chains):
            c = dirs[d]["order"][step]
            r0 = c * DN_CHUNK
            wq = jnp.concatenate([w[i][r0:r0 + DN_CHUNK], q_dec[i][r0:r0 + DN_CHUNK]], axis=0)
            res = _dot(wq, sb[i])
            v_new[i][c] = u[i][r0:r0 + DN_CHUNK] - res[:DN_CHUNK]
            o_q[i][c] = res[DN_CHUNK:]
        for i, (d, h) in enumerate(chains):
            c = dirs[d]["order"][step]
            r = dirs[d]["last_row"][c]
            vn_full = jnp.concatenate(
                [v_new[i][c].astype(BF16) if cc == c else zeros_c for cc in range(nch)], axis=0)
            state[i] = state[i] * jnp.exp(g_cols[i][r:r + 1]) + _dot(kd_t[i], vn_full)
    for i in range(nc):
        s_ref[i] = state[i]

    for i, (d, h) in enumerate(chains):
        lo = h * DN_HEAD_DIM
        vn_all = jnp.concatenate(v_new[i], axis=0).astype(BF16)
        o = jnp.concatenate(o_q[i], axis=0) + _dot(qkd[i], vn_all)
        refs[d][5][:, lo:lo + DN_HEAD_DIM] = o.astype(BF16)


def _delta_call(dq, dk, dv, bg, bgt, seq_len):
    n = dq.shape[0]
    batch = n // seq_len
    nb = seq_len // DN_BLK
    fwd = lambda b, i: (b * nb + i, 0)
    bwd = lambda b, i: (b * nb + nb - 1 - i, 0)
    fwd_t = lambda b, i: (0, b * nb + i)
    bwd_t = lambda b, i: (0, b * nb + nb - 1 - i)
    tok = lambda m: pl.BlockSpec((DN_BLK, DN_W), m)
    in_specs = [tok(fwd), tok(fwd), tok(fwd), pl.BlockSpec((DN_BLK, LANES), fwd),
                pl.BlockSpec((4 * N_DN_HEADS, DN_BLK), fwd_t),
                tok(bwd), tok(bwd), tok(bwd), pl.BlockSpec((DN_BLK, LANES), bwd),
                pl.BlockSpec((4 * N_DN_HEADS, DN_BLK), bwd_t)]
    return pl.pallas_call(
        _delta_kernel,
        grid=(batch, nb),
        in_specs=in_specs,
        out_specs=(tok(fwd), tok(bwd)),
        out_shape=(jax.ShapeDtypeStruct((n, DN_W), BF16), jax.ShapeDtypeStruct((n, DN_W), BF16)),
        scratch_shapes=[pltpu.VMEM((DN_DIRS * N_DN_HEADS, DN_HEAD_DIM, DN_HEAD_DIM), F32)],
        compiler_params=pltpu.CompilerParams(dimension_semantics=("arbitrary", "arbitrary"),
                                             vmem_limit_bytes=VMEM_LIMIT_BYTES),
        name="delta",
    )(dq, dk, dv, bg, bgt, dq, dk, dv, bg, bgt)


def _out_kernel(x_ref, attn_ref, of_ref, ob_ref, sz_ref, gate_ref, dng_ref, wa_ref, wb_ref, wo_ref,
                g1p_ref, g2_ref, w1_ref, w2_ref, g2p_ref, y_ref):
    o = of_ref[...].astype(F32) + ob_ref[...].astype(F32)
    dng = dng_ref[...]
    dn = jnp.concatenate([_rms(o[:, h * DN_HEAD_DIM:(h + 1) * DN_HEAD_DIM], dng)
                          for h in range(N_DN_HEADS)], axis=1)
    dn = (dn * sz_ref[...].astype(F32)).astype(BF16)
    a = _dot(attn_ref[...], wa_ref[...])
    dd = _dot(dn, wb_ref[...])
    gate = gate_ref[...].astype(F32)
    merged = (gate[:, :D_MODEL] * a + gate[:, D_MODEL:] * dd).astype(BF16)
    h1 = x_ref[...] + _rms(_dot(merged, wo_ref[...]), g1p_ref[...])
    hid = _dot(_rms(h1, g2_ref[...]).astype(BF16), w1_ref[...])
    hid = jnp.square(jnp.maximum(hid, 0.0)).astype(BF16)
    y_ref[...] = h1 + _rms(_dot(hid, w2_ref[...]), g2p_ref[...])


def _out_call(x2d, attn2d, o_f, o_b, sz, gate, dng, wa, wb, wo, g1p, g2, w1, w2, g2p):
    n = x2d.shape[0]
    tm = OUT_TM
    tok = lambda w: pl.BlockSpec((tm, w), lambda i: (i, 0))
    full = lambda a: pl.BlockSpec(a.shape, lambda i: (0, 0), pipeline_mode=pl.Buffered(1))
    return pl.pallas_call(
        _out_kernel,
        grid=(n // tm,),
        in_specs=[tok(D_MODEL), tok(ATT_Q), tok(DN_W), tok(DN_W), tok(DN_W), tok(2 * D_MODEL),
                  full(dng), full(wa), full(wb), full(wo), full(g1p), full(g2), full(w1), full(w2), full(g2p)],
        out_specs=tok(D_MODEL),
        out_shape=jax.ShapeDtypeStruct((n, D_MODEL), F32),
        compiler_params=pltpu.CompilerParams(dimension_semantics=("arbitrary",),
                                             vmem_limit_bytes=VMEM_LIMIT_BYTES),
        name="outmlp",
    )(x2d, attn2d, o_f, o_b, sz, gate, dng, wa, wb, wo, g1p, g2, w1, w2, g2p)


def _rope_table_t(seq_len):
    pos = jnp.arange(seq_len, dtype=jnp.int32)
    row_ids = (pos // GRID_W).astype(F32)
    col_ids = (pos % GRID_W).astype(F32)
    inv_freq = ROPE_THETA ** (-jnp.arange(0, ROPE_HALF, 2, dtype=F32) / ROPE_HALF)
    ang_r = inv_freq[:, None] * row_ids[None, :]
    ang_c = inv_freq[:, None] * col_ids[None, :]
    return jnp.concatenate([jnp.cos(ang_r), jnp.sin(ang_r), jnp.cos(ang_c), jnp.sin(ang_c)], axis=0)


def _pack_w_in(w_in):
    aq, ak, av, dqkv, dz, dbeta, da, gates = jnp.split(
        w_in, [ATT_Q, ATT_Q + ATT_KV, ATT_Q + 2 * ATT_KV, C_Z, C_Z + DN_W,
               C_Z + DN_W + 2 * N_DN_HEADS, C_Z + DN_W + 4 * N_DN_HEADS], axis=-1)
    pad = jnp.zeros((D_MODEL, LANES - 4 * N_DN_HEADS), w_in.dtype)
    return jnp.concatenate([aq, ak, av, dqkv, dz, gates, dbeta, da, pad], axis=-1).astype(BF16)


def _lane_row(v):
    v = v.reshape(-1).astype(F32)
    return jnp.zeros((1, LANES), F32).at[0, 2 * N_DN_HEADS:4 * N_DN_HEADS].set(v)


def _layer(x, p):
    batch, seq_len, _ = x.shape
    n = batch * seq_len
    x2d = x.reshape(n, D_MODEL)
    rope_t = _rope_table_t(seq_len)
    qt, k, vt, dq, dk, dv, sz, gate, bg, bgt = _proj_call(
        x2d, seq_len, p["g1"], p["w_pack"], rope_t, p["qg_t"], p["kg_t"], p["conv_w"], p["alog"], p["dtb"])
    attn = _attn_call(qt, k, vt).reshape(n, ATT_Q)
    o_f, o_b = _delta_call(dq, dk, dv, bg, bgt, seq_len)
    y = _out_call(x2d, attn, o_f, o_b, sz, gate, p["dng"], p["wa"], p["wb"], p["wo"],
                  p["g1p"], p["g2"], p["w1"], p["w2"], p["g2p"])
    return y.reshape(batch, seq_len, D_MODEL)


def kernel(x_prompt, x_sample, ln1_pre_g, w_in, attn_q_norm_g, attn_k_norm_g, dn_conv_w, dn_A_log, dn_dt_bias, dn_out_norm_g, w_attn_branch, w_dn_branch, w_out, ln1_post_g, ln2_pre_g, w_ff_in, w_ff_out, ln2_post_g):
    depth = w_in.shape[0]
    outs = []
    for x in (x_prompt, x_sample):
        for l in range(depth):
            p = {
                "g1": ln1_pre_g[l].reshape(1, D_MODEL),
                "w_pack": _pack_w_in(w_in[l]),
                "qg_t": jnp.broadcast_to(attn_q_norm_g[l][:, None], (HEAD_DIM, PROJ_TM)),
                "kg_t": jnp.broadcast_to(attn_k_norm_g[l][:, None], (HEAD_DIM, PROJ_TM)),
                "conv_w": dn_conv_w[l],
                "alog": _lane_row(dn_A_log[l]),
                "dtb": _lane_row(dn_dt_bias[l]),
                "dng": dn_out_norm_g[l].reshape(1, DN_HEAD_DIM),
                "wa": w_attn_branch[l].astype(BF16),
                "wb": w_dn_branch[l].astype(BF16),
                "wo": w_out[l].astype(BF16),
                "g1p": ln1_post_g[l].reshape(1, D_MODEL),
                "g2": ln2_pre_g[l].reshape(1, D_MODEL),
                "w1": w_ff_in[l].astype(BF16),
                "w2": w_ff_out[l].astype(BF16),
                "g2p": ln2_post_g[l].reshape(1, D_MODEL),
            }
            x = _layer(x, p)
        outs.append(x)
    return tuple(outs)
```

```python
import functools
import math

import jax
import jax.numpy as jnp
from jax import lax
from jax.experimental import pallas as pl
from jax.experimental.pallas import tpu as pltpu

D_MODEL = 1024
GRID_W = 64
N_Q_HEADS = 8
N_KV_HEADS = 2
HEAD_DIM = 64
GQA_GROUP = N_Q_HEADS // N_KV_HEADS
ROPE_HALF = HEAD_DIM // 2
ROPE_FREQS = ROPE_HALF // 2
ROPE_THETA = 10000.0
N_DN_HEADS = 4
DN_HEAD_DIM = 128
DN_CHUNK = 64
D_FF = 4 * D_MODEL
EPS = 1e-6
ATT_Q = N_Q_HEADS * HEAD_DIM
ATT_KV = N_KV_HEADS * HEAD_DIM
DN_W = N_DN_HEADS * DN_HEAD_DIM

LANES = 128
SUBLANES = 8
VMEM_LIMIT_BYTES = 56 * 1024 * 1024

C_Q = 0
C_K = C_Q + ATT_Q
C_V = C_K + ATT_KV
C_D = C_V + ATT_KV
C_Z = C_D + 3 * DN_W
C_G = C_Z + DN_W
C_B = C_G + 2 * D_MODEL
PACK_W = C_B + LANES

PROJ_TM = 512
OUT_TM = 512
ATT_TQ = 512
ATT_KVT = 512
ATT_VT = 256
ATT_SLOTS = 2
DN_BLK = 4 * DN_CHUNK
DN_DIRS = 2
DN_SEQS = 2

F32 = jnp.float32
BF16 = jnp.bfloat16
NEG_BIG = -1e30


def _rms(x, g):
    ms = jnp.mean(x * x, axis=-1, keepdims=True)
    return x * lax.rsqrt(ms + EPS) * g


def _sigmoid(x):
    return 1.0 / (1.0 + jnp.exp2(x * -math.log2(math.e)))


def _dot(a, b):
    return jnp.dot(a, b, preferred_element_type=F32)


def _dot_nt(a, b):
    return lax.dot_general(a, b, (((1,), (1,)), ((), ())), preferred_element_type=F32)


def _norm_rope_t(xt, gain_t, rope_t):
    ms = jnp.mean(xt * xt, axis=0, keepdims=True)
    xt = xt * lax.rsqrt(ms + EPS) * gain_t
    f = ROPE_FREQS
    x1r, x2r, x1c, x2c = xt[0:f], xt[f:2 * f], xt[2 * f:3 * f], xt[3 * f:4 * f]
    cr, sr, cc, sc = rope_t[0:f], rope_t[f:2 * f], rope_t[2 * f:3 * f], rope_t[3 * f:4 * f]
    return jnp.concatenate([x1r * cr - x2r * sr, x2r * cr + x1r * sr,
                            x1c * cc - x2c * sc, x2c * cc + x1c * sc], axis=0)


def _proj_kernel(x_ref, xp_ref, xn_ref, g1_ref, w_ref, rope_ref, qg_ref, kg_ref, conv_ref,
                 alog_ref, dtb_ref,
                 qt_ref, k_ref, vt_ref, dq_ref, dk_ref, dv_ref, sz_ref, gate_ref, bg_ref, bgt_ref,
                 *, tiles_per_seq):
    tm = x_ref.shape[0]
    ti = pl.program_id(0) % tiles_per_seq
    g1 = g1_ref[...]
    xb = _rms(x_ref[...], g1).astype(BF16)

    rope_t = rope_ref[...]
    yq_t = _dot(xb, w_ref[:, C_Q:C_Q + ATT_Q]).T
    q_scale = (HEAD_DIM ** -0.5) * math.log2(math.e)
    qg = qg_ref[...]
    for h in range(N_Q_HEADS):
        qh = _norm_rope_t(yq_t[h * HEAD_DIM:(h + 1) * HEAD_DIM], qg, rope_t)
        qt_ref[0, h] = (qh * q_scale).astype(BF16)

    yk_t = _dot(xb, w_ref[:, C_K:C_K + ATT_KV]).T
    kg = kg_ref[...]
    k_t = jnp.concatenate([_norm_rope_t(yk_t[h * HEAD_DIM:(h + 1) * HEAD_DIM], kg, rope_t)
                           for h in range(N_KV_HEADS)], axis=0)
    k_ref[0] = k_t.T.astype(BF16)
    yv_t = _dot(xb, w_ref[:, C_V:C_V + ATT_KV]).T.astype(BF16)
    for c in range(tm // ATT_VT):
        vt_ref[0, c] = yv_t[:, c * ATT_VT:(c + 1) * ATT_VT]

    halo = _rms(jnp.concatenate([xn_ref[...], xp_ref[...]], axis=0), g1).astype(BF16)
    y_ext = _dot(jnp.concatenate([xb, halo], axis=0), w_ref[:, C_D:C_D + 3 * DN_W])
    y = y_ext[0:tm]
    y_nxt = jnp.where(ti == tiles_per_seq - 1, 0.0, y_ext[tm:tm + SUBLANES])
    y_prv = jnp.where(ti == 0, 0.0, y_ext[tm + SUBLANES:])
    y_ext = jnp.concatenate([y, y_nxt, y_prv], axis=0)
    rows = tm + 2 * SUBLANES
    cw = conv_ref[...]
    c = (cw[0:1] * pltpu.roll(y_ext, 1, axis=0)[0:tm] + cw[1:2] * y
         + cw[2:3] * pltpu.roll(y_ext, rows - 1, axis=0)[0:tm])
    s = c * _sigmoid(c)
    for part, ref, scale in ((0, dq_ref, DN_HEAD_DIM ** -0.5), (1, dk_ref, 1.0)):
        outs = []
        for h in range(N_DN_HEADS):
            lo = part * DN_W + h * DN_HEAD_DIM
            xh = s[:, lo:lo + DN_HEAD_DIM]
            ss = jnp.sum(xh * xh, axis=-1, keepdims=True)
            outs.append(xh * (lax.rsqrt(ss + EPS) * scale))
        ref[...] = jnp.concatenate(outs, axis=1).astype(BF16)
    dv_ref[...] = s[:, 2 * DN_W:3 * DN_W].astype(BF16)

    z = _dot(xb, w_ref[:, C_Z:C_Z + DN_W])
    sz_ref[...] = (z * _sigmoid(z)).astype(BF16)
    gate_ref[...] = _sigmoid(_dot(xb, w_ref[:, C_G:C_G + 2 * D_MODEL])).astype(BF16)

    yb = _dot(xb, w_ref[:, C_B:C_B + LANES])
    lane = lax.broadcasted_iota(jnp.int32, (1, LANES), 1)
    t = yb + dtb_ref[...]
    softplus = jnp.maximum(t, 0.0) + jnp.log1p(jnp.exp(-jnp.abs(t)))
    bg = jnp.where(lane < 2 * N_DN_HEADS, _sigmoid(yb), -jnp.exp(alog_ref[...]) * softplus)
    bg = jnp.where(lane < 4 * N_DN_HEADS, bg, 0.0)
    bg_ref[...] = bg
    bgt_ref[0] = bg.T[0:4 * N_DN_HEADS]


def _proj_call(x2d, seq_len, g1, w_pack, rope_t, qg_t, kg_t, conv_w, alog_row, dtb_row):
    n = x2d.shape[0]
    tm = PROJ_TM
    batch = n // seq_len
    tps = seq_len // tm
    hb = tm // SUBLANES
    n8 = n // SUBLANES
    const = lambda i: (0, 0)
    tok = lambda i: (i, 0)
    in_specs = [
        pl.BlockSpec((tm, D_MODEL), tok),
        pl.BlockSpec((SUBLANES, D_MODEL), lambda i: (jnp.maximum(i * hb - 1, 0), 0)),
        pl.BlockSpec((SUBLANES, D_MODEL), lambda i: (jnp.minimum((i + 1) * hb, n8 - 1), 0)),
        pl.BlockSpec((1, D_MODEL), const),
        pl.BlockSpec((D_MODEL, PACK_W), const, pipeline_mode=pl.Buffered(1)),
        pl.BlockSpec((HEAD_DIM, tm), lambda i: (0, i % tps)),
        pl.BlockSpec((HEAD_DIM, tm), const),
        pl.BlockSpec((HEAD_DIM, tm), const),
        pl.BlockSpec((3, 3 * DN_W), const),
        pl.BlockSpec((1, LANES), const),
        pl.BlockSpec((1, LANES), const),
    ]
    out_shape = (
        jax.ShapeDtypeStruct((batch, N_Q_HEADS, HEAD_DIM, seq_len), BF16),
        jax.ShapeDtypeStruct((batch, seq_len, ATT_KV), BF16),
        jax.ShapeDtypeStruct((batch, seq_len // ATT_VT, ATT_KV, ATT_VT), BF16),
        jax.ShapeDtypeStruct((n, DN_W), BF16),
        jax.ShapeDtypeStruct((n, DN_W), BF16),
        jax.ShapeDtypeStruct((n, DN_W), BF16),
        jax.ShapeDtypeStruct((n, DN_W), BF16),
        jax.ShapeDtypeStruct((n, 2 * D_MODEL), BF16),
        jax.ShapeDtypeStruct((n, LANES), F32),
        jax.ShapeDtypeStruct((batch, 4 * N_DN_HEADS, seq_len), F32),
    )
    kvc = tm // ATT_VT
    out_specs = (
        pl.BlockSpec((1, N_Q_HEADS, HEAD_DIM, tm), lambda i: (i // tps, 0, 0, i % tps)),
        pl.BlockSpec((1, tm, ATT_KV), lambda i: (i // tps, i % tps, 0)),
        pl.BlockSpec((1, kvc, ATT_KV, ATT_VT), lambda i: (i // tps, i % tps, 0, 0)),
        pl.BlockSpec((tm, DN_W), tok),
        pl.BlockSpec((tm, DN_W), tok),
        pl.BlockSpec((tm, DN_W), tok),
        pl.BlockSpec((tm, DN_W), tok),
        pl.BlockSpec((tm, 2 * D_MODEL), tok),
        pl.BlockSpec((tm, LANES), tok),
        pl.BlockSpec((1, 4 * N_DN_HEADS, tm), lambda i: (i // tps, 0, i % tps)),
    )
    return pl.pallas_call(
        functools.partial(_proj_kernel, tiles_per_seq=tps),
        grid=(n // tm,),
        in_specs=in_specs,
        out_specs=out_specs,
        out_shape=out_shape,
        compiler_params=pltpu.CompilerParams(dimension_semantics=("arbitrary",),
                                             vmem_limit_bytes=VMEM_LIMIT_BYTES),
        name="proj",
    )(x2d, x2d, x2d, g1, w_pack, rope_t, qg_t, kg_t, conv_w, alog_row, dtb_row)


def _attn_kernel(q_ref, k_ref, v_ref, o_ref, s_scr, acc_scr, *, n_kv_tiles):
    kvh = pl.program_id(1)
    tq = q_ref.shape[3]
    nq = GQA_GROUP * tq
    q4 = q_ref[0]
    qcat = jnp.concatenate([q4[g] for g in range(GQA_GROUP)], axis=1).astype(F32)
    zero = jnp.zeros_like(qcat)
    qpad = jnp.where(kvh == 0, jnp.concatenate([qcat, zero], axis=0),
                     jnp.concatenate([zero, qcat], axis=0)).astype(BF16)
    vt_per_tile = ATT_KVT // ATT_VT

    def scores(slot, j):
        kt = k_ref[0, pl.ds(pl.multiple_of(j * ATT_KVT, ATT_KVT), ATT_KVT), :]
        s = _dot(kt, qpad)
        s_scr[slot] = s
        return jnp.max(s, axis=0, keepdims=True)

    def update(slot, j, m, l, tile_max):
        m_new = jnp.maximum(m, tile_max)
        alpha = jnp.exp2(m - m_new)
        p = jnp.exp2(s_scr[slot] - m_new)
        l = alpha * l + jnp.sum(p, axis=0, keepdims=True)
        vt = jnp.concatenate([v_ref[0, j * vt_per_tile + c] for c in range(vt_per_tile)], axis=1)
        acc_scr[...] = alpha * acc_scr[...] + _dot(vt, p.astype(BF16))
        return m_new, l

    acc_scr[...] = jnp.zeros_like(acc_scr)
    max0 = scores(0, 0)

    def pair(j0, m, l, max0, produce_next):
        max1 = scores(1, j0 + 1)
        m, l = update(0, j0, m, l, max0)
        if produce_next:
            max0 = scores(0, j0 + 2)
        m, l = update(1, j0 + 1, m, l, max1)
        return m, l, max0

    init = (jnp.full((1, nq), NEG_BIG, F32), jnp.zeros((1, nq), F32), max0)
    n_pairs = n_kv_tiles // ATT_SLOTS
    m, l, max0 = lax.fori_loop(0, n_pairs - 1, lambda jj, c: pair(ATT_SLOTS * jj, *c, True), init)
    _, l, _ = pair(ATT_SLOTS * (n_pairs - 1), m, l, max0, False)
    out = acc_scr[...] * (1.0 / l)
    out = jnp.concatenate([out[:, g * tq:(g + 1) * tq] for g in range(GQA_GROUP)], axis=0)
    o_ref[0] = out.T.astype(BF16)


def _attn_call(qt, k, vt):
    batch, _, _, seq_len = qt.shape
    tq = ATT_TQ
    nkv = seq_len // ATT_KVT
    assert nkv % ATT_SLOTS == 0
    nq = GQA_GROUP * tq
    return pl.pallas_call(
        functools.partial(_attn_kernel, n_kv_tiles=nkv),
        grid=(batch, N_KV_HEADS, seq_len // tq),
        in_specs=[
            pl.BlockSpec((1, GQA_GROUP, HEAD_DIM, tq), lambda b, h, i: (b, h, 0, i)),
            pl.BlockSpec((1, seq_len, ATT_KV), lambda b, h, i: (b, 0, 0)),
            pl.BlockSpec((1, seq_len // ATT_VT, HEAD_DIM, ATT_VT), lambda b, h, i: (b, 0, h, 0)),
        ],
        out_specs=pl.BlockSpec((1, tq, GQA_GROUP * HEAD_DIM), lambda b, h, i: (b, i, h)),
        out_shape=jax.ShapeDtypeStruct((batch, seq_len, ATT_Q), BF16),
        scratch_shapes=[pltpu.VMEM((ATT_SLOTS, ATT_KVT, nq), F32), pltpu.VMEM((HEAD_DIM, nq), F32)],
        compiler_params=pltpu.CompilerParams(
            dimension_semantics=("arbitrary", "arbitrary", "arbitrary"),
            vmem_limit_bytes=VMEM_LIMIT_BYTES),
        name="attn",
    )(qt, k, vt)


def _split3(x):
    p1 = x.astype(BF16)
    r1 = x - p1.astype(F32)
    p2 = r1.astype(BF16)
    p3 = (r1 - p2.astype(F32)).astype(BF16)
    return p1, p2, p3


def _delta_kernel(qf_ref, kf_ref, vf_ref, bgf_ref, bgtf_ref, qb_ref, kb_ref, vb_ref, bgb_ref, bgtb_ref,
                  of_ref, ob_ref, s_ref):
    @pl.when(pl.program_id(1) == 0)
    def _():
        s_ref[...] = jnp.zeros_like(s_ref)

    blk = DN_BLK
    nch = blk // DN_CHUNK
    ri = lax.broadcasted_iota(jnp.int32, (blk, blk), 0)
    ci = lax.broadcasted_iota(jnp.int32, (blk, blk), 1)
    same = (ri // DN_CHUNK) == (ci // DN_CHUNK)
    lower_incl, lower_strict = same & (ri >= ci), same & (ri > ci)
    upper_incl, upper_strict = same & (ri <= ci), same & (ri < ci)
    eye = (ri == ci).astype(F32)
    dir_refs = ((qf_ref, kf_ref, vf_ref, bgf_ref, bgtf_ref, of_ref),
                (qb_ref, kb_ref, vb_ref, bgb_ref, bgtb_ref, ob_ref))
    n_streams = DN_SEQS * DN_DIRS
    refs = [tuple(r.at[0, d // DN_DIRS] for r in dir_refs[d % DN_DIRS]) for d in range(n_streams)]

    dirs = []
    for d in range(n_streams):
        reverse = d % DN_DIRS == 1
        incl, strict, incl_t = ((upper_incl, upper_strict, lower_incl) if reverse
                                else (lower_incl, lower_strict, upper_incl))
        bg = refs[d][3][...]
        bgt = refs[d][4][...]
        m_col = jnp.where(incl, 1.0, 0.0).astype(BF16)
        m_row = jnp.where(incl_t, 1.0, 0.0).astype(BF16)
        gc = sum(_dot(m_col, piece) for piece in _split3(bg))
        gct = sum(_dot(piece, m_row) for piece in _split3(bgt))
        last_row = [(c * DN_CHUNK if reverse else (c + 1) * DN_CHUNK - 1) for c in range(nch)]
        order = list(range(nch - 1, -1, -1)) if reverse else list(range(nch))
        dirs.append(dict(incl=incl, strict=strict, bg=bg, gc=gc, gct=gct, last_row=last_row, order=order))

    chains = [(d, h) for d in range(n_streams) for h in range(N_DN_HEADS)]
    nc = len(chains)

    qh, kh, vh, kk, qk = [], [], [], [], []
    for d, h in chains:
        lo = h * DN_HEAD_DIM
        qh.append(refs[d][0][:, lo:lo + DN_HEAD_DIM])
        kh.append(refs[d][1][:, lo:lo + DN_HEAD_DIM])
        vh.append(refs[d][2][:, lo:lo + DN_HEAD_DIM])
    for i in range(nc):
        kk.append(_dot_nt(kh[i], kh[i]))
        qk.append(_dot_nt(qh[i], kh[i]))

    low, qkd, rhs, q_dec, kd_t, g_cols = [], [], [], [], [], []
    for i, (d, h) in enumerate(chains):
        dd = dirs[d]
        cb = (d % DN_DIRS) * N_DN_HEADS + h
        cg = 2 * N_DN_HEADS + cb
        b_col = dd["bg"][:, cb:cb + 1]
        g_col = dd["gc"][:, cg:cg + 1]
        g_row = dd["gct"][cg:cg + 1, :]
        decay = jnp.exp(jnp.where(dd["incl"], g_col - g_row, NEG_BIG))
        low.append(jnp.where(dd["strict"], b_col * kk[i] * decay, 0.0))
        qkd.append((qk[i] * decay).astype(BF16))
        e_col = jnp.exp(g_col)
        kf = kh[i].astype(F32)
        rhs.append(jnp.concatenate([vh[i].astype(F32) * b_col, kf * (b_col * e_col)], axis=1).astype(BF16))
        q_dec.append((qh[i].astype(F32) * e_col).astype(BF16))
        gl_col = jnp.concatenate(
            [jnp.broadcast_to(g_col[r:r + 1], (DN_CHUNK, 1)) for r in dd["last_row"]], axis=0)
        kd_t.append((kf * jnp.exp(gl_col - g_col)).T.astype(BF16))
        g_cols.append(g_col)

    def compact(x):
        return sum(x[c * DN_CHUNK:(c + 1) * DN_CHUNK] for c in range(nch))

    def expand(xc):
        return jnp.where(same, jnp.concatenate([xc] * nch, axis=0), 0.0).astype(BF16)

    eye_c = compact(eye)
    low_c = [compact(x) for x in low]
    xc = [_dot(low_c[i].astype(BF16), low[i].astype(BF16)) for i in range(nc)]
    rc = [eye_c - low_c[i] for i in range(nc)]
    n_sq = int(math.log2(DN_CHUNK)) - 1
    for it in range(n_sq):
        xbd = [expand(x) for x in xc]
        if it < n_sq - 1:
            both = [_dot(jnp.concatenate([rc[i], xc[i]], axis=0).astype(BF16), xbd[i]) for i in range(nc)]
            rc = [rc[i] + both[i][:DN_CHUNK] for i in range(nc)]
            xc = [both[i][DN_CHUNK:] for i in range(nc)]
        else:
            rc = [rc[i] + _dot(rc[i].astype(BF16), xbd[i]) for i in range(nc)]

    uw = [_dot(expand(rc[i]), rhs[i]) for i in range(nc)]
    u = [x[:, :DN_HEAD_DIM] for x in uw]
    w = [x[:, DN_HEAD_DIM:].astype(BF16) for x in uw]

    state = [s_ref[i] for i in range(nc)]
    v_new = [[None] * nch for _ in range(nc)]
    o_q = [[None] * nch for _ in range(nc)]
    zeros_c = jnp.zeros((DN_CHUNK, DN_HEAD_DIM), BF16)
    for step in range(nch):
        sb = [x.astype(BF16) for x in state]
        for i, (d, h) in enumerate(chains):
            c = dirs[d]["order"][step]
            r0 = c * DN_CHUNK
            wq = jnp.concatenate([w[i][r0:r0 + DN_CHUNK], q_dec[i][r0:r0 + DN_CHUNK]], axis=0)
            res = _dot(wq, sb[i])
            v_new[i][c] = u[i][r0:r0 + DN_CHUNK] - res[:DN_CHUNK]
            o_q[i][c] = res[DN_CHUNK:]
        for i, (d, h) in enumerate(chains):
            c = dirs[d]["order"][step]
            r = dirs[d]["last_row"][c]
            vn_full = jnp.concatenate(
                [v_new[i][c].astype(BF16) if cc == c else zeros_c for cc in range(nch)], axis=0)
            state[i] = state[i] * jnp.exp(g_cols[i][r:r + 1]) + _dot(kd_t[i], vn_full)
    for i in range(nc):
        s_ref[i] = state[i]

    for i, (d, h) in enumerate(chains):
        lo = h * DN_HEAD_DIM
        vn_all = jnp.concatenate(v_new[i], axis=0).astype(BF16)
        o = jnp.concatenate(o_q[i], axis=0) + _dot(qkd[i], vn_all)
        refs[d][5][:, lo:lo + DN_HEAD_DIM] = o.astype(BF16)


def _delta_call(dq, dk, dv, bg, bgt, seq_len):
    n = dq.shape[0]
    batch = n // seq_len
    nb = seq_len // DN_BLK
    groups = batch // DN_SEQS
    dq, dk, dv = (a.reshape(groups, DN_SEQS, seq_len, DN_W) for a in (dq, dk, dv))
    bg = bg.reshape(groups, DN_SEQS, seq_len, LANES)
    bgt = bgt.reshape(groups, DN_SEQS, 4 * N_DN_HEADS, seq_len)
    fwd = lambda b, i: (b, 0, i, 0)
    bwd = lambda b, i: (b, 0, nb - 1 - i, 0)
    fwd_t = lambda b, i: (b, 0, 0, i)
    bwd_t = lambda b, i: (b, 0, 0, nb - 1 - i)
    tok = lambda m: pl.BlockSpec((1, DN_SEQS, DN_BLK, DN_W), m)
    in_specs = [tok(fwd), tok(fwd), tok(fwd), pl.BlockSpec((1, DN_SEQS, DN_BLK, LANES), fwd),
                pl.BlockSpec((1, DN_SEQS, 4 * N_DN_HEADS, DN_BLK), fwd_t),
                tok(bwd), tok(bwd), tok(bwd), pl.BlockSpec((1, DN_SEQS, DN_BLK, LANES), bwd),
                pl.BlockSpec((1, DN_SEQS, 4 * N_DN_HEADS, DN_BLK), bwd_t)]
    o_shape = jax.ShapeDtypeStruct((groups, DN_SEQS, seq_len, DN_W), BF16)
    o_f, o_b = pl.pallas_call(
        _delta_kernel,
        grid=(groups, nb),
        in_specs=in_specs,
        out_specs=(tok(fwd), tok(bwd)),
        out_shape=(o_shape, o_shape),
        scratch_shapes=[pltpu.VMEM((DN_SEQS * DN_DIRS * N_DN_HEADS, DN_HEAD_DIM, DN_HEAD_DIM), F32)],
        compiler_params=pltpu.CompilerParams(dimension_semantics=("arbitrary", "arbitrary"),
                                             vmem_limit_bytes=VMEM_LIMIT_BYTES),
        name="delta",
    )(dq, dk, dv, bg, bgt, dq, dk, dv, bg, bgt)
    return o_f.reshape(n, DN_W), o_b.reshape(n, DN_W)


def _out_kernel(x_ref, attn_ref, of_ref, ob_ref, sz_ref, gate_ref, dng_ref, wa_ref, wb_ref, wo_ref,
                g1p_ref, g2_ref, w1_ref, w2_ref, g2p_ref, y_ref):
    o = of_ref[...].astype(F32) + ob_ref[...].astype(F32)
    dng = dng_ref[...]
    dn = jnp.concatenate([_rms(o[:, h * DN_HEAD_DIM:(h + 1) * DN_HEAD_DIM], dng)
                          for h in range(N_DN_HEADS)], axis=1)
    dn = (dn * sz_ref[...].astype(F32)).astype(BF16)
    a = _dot(attn_ref[...], wa_ref[...])
    dd = _dot(dn, wb_ref[...])
    gate = gate_ref[...].astype(F32)
    merged = (gate[:, :D_MODEL] * a + gate[:, D_MODEL:] * dd).astype(BF16)
    h1 = x_ref[...] + _rms(_dot(merged, wo_ref[...]), g1p_ref[...])
    hid = _dot(_rms(h1, g2_ref[...]).astype(BF16), w1_ref[...])
    hid = jnp.square(jnp.maximum(hid, 0.0)).astype(BF16)
    y_ref[...] = h1 + _rms(_dot(hid, w2_ref[...]), g2p_ref[...])


def _out_call(x2d, attn2d, o_f, o_b, sz, gate, dng, wa, wb, wo, g1p, g2, w1, w2, g2p):
    n = x2d.shape[0]
    tm = OUT_TM
    tok = lambda w: pl.BlockSpec((tm, w), lambda i: (i, 0))
    full = lambda a: pl.BlockSpec(a.shape, lambda i: (0, 0), pipeline_mode=pl.Buffered(1))
    return pl.pallas_call(
        _out_kernel,
        grid=(n // tm,),
        in_specs=[tok(D_MODEL), tok(ATT_Q), tok(DN_W), tok(DN_W), tok(DN_W), tok(2 * D_MODEL),
                  full(dng), full(wa), full(wb), full(wo), full(g1p), full(g2), full(w1), full(w2), full(g2p)],
        out_specs=tok(D_MODEL),
        out_shape=jax.ShapeDtypeStruct((n, D_MODEL), F32),
        compiler_params=pltpu.CompilerParams(dimension_semantics=("arbitrary",),
                                             vmem_limit_bytes=VMEM_LIMIT_BYTES),
        name="outmlp",
    )(x2d, attn2d, o_f, o_b, sz, gate, dng, wa, wb, wo, g1p, g2, w1, w2, g2p)


def _rope_table_t(seq_len):
    pos = jnp.arange(seq_len, dtype=jnp.int32)
    row_ids = (pos // GRID_W).astype(F32)
    col_ids = (pos % GRID_W).astype(F32)
    inv_freq = ROPE_THETA ** (-jnp.arange(0, ROPE_HALF, 2, dtype=F32) / ROPE_HALF)
    ang_r = inv_freq[:, None] * row_ids[None, :]
    ang_c = inv_freq[:, None] * col_ids[None, :]
    return jnp.concatenate([jnp.cos(ang_r), jnp.sin(ang_r), jnp.cos(ang_c), jnp.sin(ang_c)], axis=0)


def _pack_w_in(w_in):
    aq, ak, av, dqkv, dz, dbeta, da, gates = jnp.split(
        w_in, [ATT_Q, ATT_Q + ATT_KV, ATT_Q + 2 * ATT_KV, C_Z, C_Z + DN_W,
               C_Z + DN_W + 2 * N_DN_HEADS, C_Z + DN_W + 4 * N_DN_HEADS], axis=-1)
    pad = jnp.zeros((D_MODEL, LANES - 4 * N_DN_HEADS), w_in.dtype)
    return jnp.concatenate([aq, ak, av, dqkv, dz, gates, dbeta, da, pad], axis=-1).astype(BF16)


def _lane_row(v):
    v = v.reshape(-1).astype(F32)
    return jnp.zeros((1, LANES), F32).at[0, 2 * N_DN_HEADS:4 * N_DN_HEADS].set(v)


def _layer(x, p):
    batch, seq_len, _ = x.shape
    n = batch * seq_len
    x2d = x.reshape(n, D_MODEL)
    rope_t = _rope_table_t(seq_len)
    qt, k, vt, dq, dk, dv, sz, gate, bg, bgt = _proj_call(
        x2d, seq_len, p["g1"], p["w_pack"], rope_t, p["qg_t"], p["kg_t"], p["conv_w"], p["alog"], p["dtb"])
    attn = _attn_call(qt, k, vt).reshape(n, ATT_Q)
    o_f, o_b = _delta_call(dq, dk, dv, bg, bgt, seq_len)
    y = _out_call(x2d, attn, o_f, o_b, sz, gate, p["dng"], p["wa"], p["wb"], p["wo"],
                  p["g1p"], p["g2"], p["w1"], p["w2"], p["g2p"])
    return y.reshape(batch, seq_len, D_MODEL)


def kernel(x_prompt, x_sample, ln1_pre_g, w_in, attn_q_norm_g, attn_k_norm_g, dn_conv_w, dn_A_log, dn_dt_bias, dn_out_norm_g, w_attn_branch, w_dn_branch, w_out, ln1_post_g, ln2_pre_g, w_ff_in, w_ff_out, ln2_post_g):
    depth = w_in.shape[0]
    outs = []
    for x in (x_prompt, x_sample):
        for l in range(depth):
            p = {
                "g1": ln1_pre_g[l].reshape(1, D_MODEL),
                "w_pack": _pack_w_in(w_in[l]),
                "qg_t": jnp.broadcast_to(attn_q_norm_g[l][:, None], (HEAD_DIM, PROJ_TM)),
                "kg_t": jnp.broadcast_to(attn_k_norm_g[l][:, None], (HEAD_DIM, PROJ_TM)),
                "conv_w": dn_conv_w[l],
                "alog": _lane_row(dn_A_log[l]),
                "dtb": _lane_row(dn_dt_bias[l]),
                "dng": dn_out_norm_g[l].reshape(1, DN_HEAD_DIM),
                "wa": w_attn_branch[l].astype(BF16),
                "wb": w_dn_branch[l].astype(BF16),
                "wo": w_out[l].astype(BF16),
                "g1p": ln1_post_g[l].reshape(1, D_MODEL),
                "g2": ln2_pre_g[l].reshape(1, D_MODEL),
                "w1": w_ff_in[l].astype(BF16),
                "w2": w_ff_out[l].astype(BF16),
                "g2p": ln2_post_g[l].reshape(1, D_MODEL),
            }
            x = _layer(x, p)
        outs.append(x)
    return tuple(outs)
```

```python
import functools
import math

import jax
import jax.numpy as jnp
from jax import lax
from jax.experimental import pallas as pl
from jax.experimental.pallas import tpu as pltpu

D_MODEL = 1024
GRID_W = 64
N_Q_HEADS = 8
N_KV_HEADS = 2
HEAD_DIM = 64
GQA_GROUP = N_Q_HEADS // N_KV_HEADS
ROPE_HALF = HEAD_DIM // 2
ROPE_FREQS = ROPE_HALF // 2
ROPE_THETA = 10000.0
N_DN_HEADS = 4
DN_HEAD_DIM = 128
DN_CHUNK = 64
D_FF = 4 * D_MODEL
EPS = 1e-6
ATT_Q = N_Q_HEADS * HEAD_DIM
ATT_KV = N_KV_HEADS * HEAD_DIM
DN_W = N_DN_HEADS * DN_HEAD_DIM

LANES = 128
SUBLANES = 8
VMEM_LIMIT_BYTES = 56 * 1024 * 1024

C_Q = 0
C_K = C_Q + ATT_Q
C_V = C_K + ATT_KV
C_D = C_V + ATT_KV
C_Z = C_D + 3 * DN_W
C_G = C_Z + DN_W
C_B = C_G + 2 * D_MODEL
PACK_W = C_B + LANES

PROJ_TM = 512
OUT_TM = 512
ATT_TQ = 512
ATT_KVT = 1024
ATT_VT = 256
ATT_SLOTS = 2
DN_BLK = 4 * DN_CHUNK
DN_DIRS = 2
DN_SEQS = 2

F32 = jnp.float32
BF16 = jnp.bfloat16
NEG_BIG = -1e30


def _rms(x, g):
    ms = jnp.mean(x * x, axis=-1, keepdims=True)
    return x * lax.rsqrt(ms + EPS) * g


def _sigmoid(x):
    return 1.0 / (1.0 + jnp.exp2(x * -math.log2(math.e)))


def _dot(a, b):
    return jnp.dot(a, b, preferred_element_type=F32)


def _dot_nt(a, b):
    return lax.dot_general(a, b, (((1,), (1,)), ((), ())), preferred_element_type=F32)


def _norm_rope_t(xt, gain_t, rope_t):
    ms = jnp.mean(xt * xt, axis=0, keepdims=True)
    xt = xt * lax.rsqrt(ms + EPS) * gain_t
    f = ROPE_FREQS
    x1r, x2r, x1c, x2c = xt[0:f], xt[f:2 * f], xt[2 * f:3 * f], xt[3 * f:4 * f]
    cr, sr, cc, sc = rope_t[0:f], rope_t[f:2 * f], rope_t[2 * f:3 * f], rope_t[3 * f:4 * f]
    return jnp.concatenate([x1r * cr - x2r * sr, x2r * cr + x1r * sr,
                            x1c * cc - x2c * sc, x2c * cc + x1c * sc], axis=0)


def _proj_kernel(x_ref, xp_ref, xn_ref, g1_ref, w_ref, rope_ref, qg_ref, kg_ref, conv_ref,
                 alog_ref, dtb_ref,
                 qt_ref, k_ref, vt_ref, dq_ref, dk_ref, dv_ref, sz_ref, gate_ref, bg_ref, bgt_ref,
                 *, tiles_per_seq):
    tm = x_ref.shape[0]
    ti = pl.program_id(0) % tiles_per_seq
    g1 = g1_ref[...]
    xb = _rms(x_ref[...], g1).astype(BF16)

    rope_t = rope_ref[...]
    yq_t = _dot(xb, w_ref[:, C_Q:C_Q + ATT_Q]).T
    q_scale = (HEAD_DIM ** -0.5) * math.log2(math.e)
    qg = qg_ref[...]
    for h in range(N_Q_HEADS):
        qh = _norm_rope_t(yq_t[h * HEAD_DIM:(h + 1) * HEAD_DIM], qg, rope_t)
        qt_ref[0, h] = (qh * q_scale).astype(BF16)

    yk_t = _dot(xb, w_ref[:, C_K:C_K + ATT_KV]).T
    kg = kg_ref[...]
    k_t = jnp.concatenate([_norm_rope_t(yk_t[h * HEAD_DIM:(h + 1) * HEAD_DIM], kg, rope_t)
                           for h in range(N_KV_HEADS)], axis=0)
    k_ref[0] = k_t.T.astype(BF16)
    yv_t = _dot(xb, w_ref[:, C_V:C_V + ATT_KV]).T.astype(BF16)
    for c in range(tm // ATT_VT):
        vt_ref[0, c] = yv_t[:, c * ATT_VT:(c + 1) * ATT_VT]

    halo = _rms(jnp.concatenate([xn_ref[...], xp_ref[...]], axis=0), g1).astype(BF16)
    y_ext = _dot(jnp.concatenate([xb, halo], axis=0), w_ref[:, C_D:C_D + 3 * DN_W])
    y = y_ext[0:tm]
    y_nxt = jnp.where(ti == tiles_per_seq - 1, 0.0, y_ext[tm:tm + SUBLANES])
    y_prv = jnp.where(ti == 0, 0.0, y_ext[tm + SUBLANES:])
    y_ext = jnp.concatenate([y, y_nxt, y_prv], axis=0)
    rows = tm + 2 * SUBLANES
    cw = conv_ref[...]
    c = (cw[0:1] * pltpu.roll(y_ext, 1, axis=0)[0:tm] + cw[1:2] * y
         + cw[2:3] * pltpu.roll(y_ext, rows - 1, axis=0)[0:tm])
    s = c * _sigmoid(c)
    for part, ref, scale in ((0, dq_ref, DN_HEAD_DIM ** -0.5), (1, dk_ref, 1.0)):
        outs = []
        for h in range(N_DN_HEADS):
            lo = part * DN_W + h * DN_HEAD_DIM
            xh = s[:, lo:lo + DN_HEAD_DIM]
            ss = jnp.sum(xh * xh, axis=-1, keepdims=True)
            outs.append(xh * (lax.rsqrt(ss + EPS) * scale))
        ref[...] = jnp.concatenate(outs, axis=1).astype(BF16)
    dv_ref[...] = s[:, 2 * DN_W:3 * DN_W].astype(BF16)

    z = _dot(xb, w_ref[:, C_Z:C_Z + DN_W])
    sz_ref[...] = (z * _sigmoid(z)).astype(BF16)
    gate_ref[...] = _sigmoid(_dot(xb, w_ref[:, C_G:C_G + 2 * D_MODEL])).astype(BF16)

    yb = _dot(xb, w_ref[:, C_B:C_B + LANES])
    lane = lax.broadcasted_iota(jnp.int32, (1, LANES), 1)
    t = yb + dtb_ref[...]
    softplus = jnp.maximum(t, 0.0) + jnp.log1p(jnp.exp(-jnp.abs(t)))
    bg = jnp.where(lane < 2 * N_DN_HEADS, _sigmoid(yb), -jnp.exp(alog_ref[...]) * softplus)
    bg = jnp.where(lane < 4 * N_DN_HEADS, bg, 0.0)
    bg_ref[...] = bg
    bgt_ref[0] = bg.T[0:4 * N_DN_HEADS]


def _proj_call(x2d, seq_len, g1, w_pack, rope_t, qg_t, kg_t, conv_w, alog_row, dtb_row):
    n = x2d.shape[0]
    tm = PROJ_TM
    batch = n // seq_len
    tps = seq_len // tm
    hb = tm // SUBLANES
    n8 = n // SUBLANES
    const = lambda i: (0, 0)
    tok = lambda i: (i, 0)
    in_specs = [
        pl.BlockSpec((tm, D_MODEL), tok),
        pl.BlockSpec((SUBLANES, D_MODEL), lambda i: (jnp.maximum(i * hb - 1, 0), 0)),
        pl.BlockSpec((SUBLANES, D_MODEL), lambda i: (jnp.minimum((i + 1) * hb, n8 - 1), 0)),
        pl.BlockSpec((1, D_MODEL), const),
        pl.BlockSpec((D_MODEL, PACK_W), const, pipeline_mode=pl.Buffered(1)),
        pl.BlockSpec((HEAD_DIM, tm), lambda i: (0, i % tps)),
        pl.BlockSpec((HEAD_DIM, tm), const),
        pl.BlockSpec((HEAD_DIM, tm), const),
        pl.BlockSpec((3, 3 * DN_W), const),
        pl.BlockSpec((1, LANES), const),
        pl.BlockSpec((1, LANES), const),
    ]
    out_shape = (
        jax.ShapeDtypeStruct((batch, N_Q_HEADS, HEAD_DIM, seq_len), BF16),
        jax.ShapeDtypeStruct((batch, seq_len, ATT_KV), BF16),
        jax.ShapeDtypeStruct((batch, seq_len // ATT_VT, ATT_KV, ATT_VT), BF16),
        jax.ShapeDtypeStruct((n, DN_W), BF16),
        jax.ShapeDtypeStruct((n, DN_W), BF16),
        jax.ShapeDtypeStruct((n, DN_W), BF16),
        jax.ShapeDtypeStruct((n, DN_W), BF16),
        jax.ShapeDtypeStruct((n, 2 * D_MODEL), BF16),
        jax.ShapeDtypeStruct((n, LANES), F32),
        jax.ShapeDtypeStruct((batch, 4 * N_DN_HEADS, seq_len), F32),
    )
    kvc = tm // ATT_VT
    out_specs = (
        pl.BlockSpec((1, N_Q_HEADS, HEAD_DIM, tm), lambda i: (i // tps, 0, 0, i % tps)),
        pl.BlockSpec((1, tm, ATT_KV), lambda i: (i // tps, i % tps, 0)),
        pl.BlockSpec((1, kvc, ATT_KV, ATT_VT), lambda i: (i // tps, i % tps, 0, 0)),
        pl.BlockSpec((tm, DN_W), tok),
        pl.BlockSpec((tm, DN_W), tok),
        pl.BlockSpec((tm, DN_W), tok),
        pl.BlockSpec((tm, DN_W), tok),
        pl.BlockSpec((tm, 2 * D_MODEL), tok),
        pl.BlockSpec((tm, LANES), tok),
        pl.BlockSpec((1, 4 * N_DN_HEADS, tm), lambda i: (i // tps, 0, i % tps)),
    )
    return pl.pallas_call(
        functools.partial(_proj_kernel, tiles_per_seq=tps),
        grid=(n // tm,),
        in_specs=in_specs,
        out_specs=out_specs,
        out_shape=out_shape,
        compiler_params=pltpu.CompilerParams(dimension_semantics=("arbitrary",),
                                             vmem_limit_bytes=VMEM_LIMIT_BYTES),
        name="proj",
    )(x2d, x2d, x2d, g1, w_pack, rope_t, qg_t, kg_t, conv_w, alog_row, dtb_row)


def _attn_kernel(q_ref, k_ref, v_ref, o_ref, s_scr, acc_scr, *, n_kv_tiles):
    kvh = pl.program_id(1)
    tq = q_ref.shape[3]
    nq = GQA_GROUP * tq
    q4 = q_ref[0]
    qcat = jnp.concatenate([q4[g] for g in range(GQA_GROUP)], axis=1).astype(F32)
    zero = jnp.zeros_like(qcat)
    qpad = jnp.where(kvh == 0, jnp.concatenate([qcat, zero], axis=0),
                     jnp.concatenate([zero, qcat], axis=0)).astype(BF16)
    vt_per_tile = ATT_KVT // ATT_VT

    def scores(slot, j):
        kt = k_ref[0, pl.ds(pl.multiple_of(j * ATT_KVT, ATT_KVT), ATT_KVT), :]
        s = _dot(kt, qpad)
        s_scr[slot] = s
        return jnp.max(s, axis=0, keepdims=True)

    def update(slot, j, m, l, tile_max):
        m_new = jnp.maximum(m, tile_max)
        alpha = jnp.exp2(m - m_new)
        p = jnp.exp2(s_scr[slot] - m_new)
        l = alpha * l + jnp.sum(p, axis=0, keepdims=True)
        vt = jnp.concatenate([v_ref[0, j * vt_per_tile + c] for c in range(vt_per_tile)], axis=1)
        acc_scr[...] = alpha * acc_scr[...] + _dot(vt, p.astype(BF16))
        return m_new, l

    acc_scr[...] = jnp.zeros_like(acc_scr)
    ahead = ATT_SLOTS // 2
    maxes = tuple(scores(t, t) for t in range(ahead))

    def group(j0, m, l, maxes, produce_next):
        maxes = list(maxes) + [None] * (ATT_SLOTS - ahead)
        for t in range(ATT_SLOTS):
            tp = t + ahead
            if tp < ATT_SLOTS or produce_next:
                maxes[tp % ATT_SLOTS] = scores(tp % ATT_SLOTS, j0 + tp)
            m, l = update(t, j0 + t, m, l, maxes[t])
        return m, l, tuple(maxes[:ahead])

    init = (jnp.full((1, nq), NEG_BIG, F32), jnp.zeros((1, nq), F32), maxes)
    n_groups = n_kv_tiles // ATT_SLOTS
    m, l, maxes = lax.fori_loop(0, n_groups - 1, lambda jj, c: group(ATT_SLOTS * jj, *c, True), init)
    _, l, _ = group(ATT_SLOTS * (n_groups - 1), m, l, maxes, False)
    out = acc_scr[...] * (1.0 / l)
    out = jnp.concatenate([out[:, g * tq:(g + 1) * tq] for g in range(GQA_GROUP)], axis=0)
    o_ref[0] = out.T.astype(BF16)


def _attn_call(qt, k, vt):
    batch, _, _, seq_len = qt.shape
    tq = ATT_TQ
    nkv = seq_len // ATT_KVT
    assert nkv % ATT_SLOTS == 0
    nq = GQA_GROUP * tq
    return pl.pallas_call(
        functools.partial(_attn_kernel, n_kv_tiles=nkv),
        grid=(batch, N_KV_HEADS, seq_len // tq),
        in_specs=[
            pl.BlockSpec((1, GQA_GROUP, HEAD_DIM, tq), lambda b, h, i: (b, h, 0, i)),
            pl.BlockSpec((1, seq_len, ATT_KV), lambda b, h, i: (b, 0, 0)),
            pl.BlockSpec((1, seq_len // ATT_VT, HEAD_DIM, ATT_VT), lambda b, h, i: (b, 0, h, 0)),
        ],
        out_specs=pl.BlockSpec((1, tq, GQA_GROUP * HEAD_DIM), lambda b, h, i: (b, i, h)),
        out_shape=jax.ShapeDtypeStruct((batch, seq_len, ATT_Q), BF16),
        scratch_shapes=[pltpu.VMEM((ATT_SLOTS, ATT_KVT, nq), F32), pltpu.VMEM((HEAD_DIM, nq), F32)],
        compiler_params=pltpu.CompilerParams(
            dimension_semantics=("arbitrary", "arbitrary", "arbitrary"),
            vmem_limit_bytes=VMEM_LIMIT_BYTES),
        name="attn",
    )(qt, k, vt)


def _split3(x):
    p1 = x.astype(BF16)
    r1 = x - p1.astype(F32)
    p2 = r1.astype(BF16)
    p3 = (r1 - p2.astype(F32)).astype(BF16)
    return p1, p2, p3


def _delta_kernel(qf_ref, kf_ref, vf_ref, bgf_ref, bgtf_ref, qb_ref, kb_ref, vb_ref, bgb_ref, bgtb_ref,
                  of_ref, ob_ref, s_ref):
    @pl.when(pl.program_id(1) == 0)
    def _():
        s_ref[...] = jnp.zeros_like(s_ref)

    blk = DN_BLK
    nch = blk // DN_CHUNK
    ri = lax.broadcasted_iota(jnp.int32, (blk, blk), 0)
    ci = lax.broadcasted_iota(jnp.int32, (blk, blk), 1)
    same = (ri // DN_CHUNK) == (ci // DN_CHUNK)
    lower_incl, lower_strict = same & (ri >= ci), same & (ri > ci)
    upper_incl, upper_strict = same & (ri <= ci), same & (ri < ci)
    eye = (ri == ci).astype(F32)
    dir_refs = ((qf_ref, kf_ref, vf_ref, bgf_ref, bgtf_ref, of_ref),
                (qb_ref, kb_ref, vb_ref, bgb_ref, bgtb_ref, ob_ref))
    n_streams = DN_SEQS * DN_DIRS
    refs = [tuple(r.at[0, d // DN_DIRS] for r in dir_refs[d % DN_DIRS]) for d in range(n_streams)]

    dirs = []
    for d in range(n_streams):
        reverse = d % DN_DIRS == 1
        incl, strict, incl_t = ((upper_incl, upper_strict, lower_incl) if reverse
                                else (lower_incl, lower_strict, upper_incl))
        bg = refs[d][3][...]
        bgt = refs[d][4][...]
        m_col = jnp.where(incl, 1.0, 0.0).astype(BF16)
        m_row = jnp.where(incl_t, 1.0, 0.0).astype(BF16)
        gc = sum(_dot(m_col, piece) for piece in _split3(bg))
        gct = sum(_dot(piece, m_row) for piece in _split3(bgt))
        last_row = [(c * DN_CHUNK if reverse else (c + 1) * DN_CHUNK - 1) for c in range(nch)]
        order = list(range(nch - 1, -1, -1)) if reverse else list(range(nch))
        dirs.append(dict(incl=incl, strict=strict, bg=bg, gc=gc, gct=gct, last_row=last_row, order=order))

    chains = [(d, h) for d in range(n_streams) for h in range(N_DN_HEADS)]
    nc = len(chains)

    qh, kh, vh, kk, qk = [], [], [], [], []
    for d, h in chains:
        lo = h * DN_HEAD_DIM
        qh.append(refs[d][0][:, lo:lo + DN_HEAD_DIM])
        kh.append(refs[d][1][:, lo:lo + DN_HEAD_DIM])
        vh.append(refs[d][2][:, lo:lo + DN_HEAD_DIM])
    for i in range(nc):
        kk.append(_dot_nt(kh[i], kh[i]))
        qk.append(_dot_nt(qh[i], kh[i]))

    low, qkd, rhs, q_dec, k_dec, g_cols = [], [], [], [], [], []
    for i, (d, h) in enumerate(chains):
        dd = dirs[d]
        cb = (d % DN_DIRS) * N_DN_HEADS + h
        cg = 2 * N_DN_HEADS + cb
        b_col = dd["bg"][:, cb:cb + 1]
        g_col = dd["gc"][:, cg:cg + 1]
        g_row = dd["gct"][cg:cg + 1, :]
        decay = jnp.exp(jnp.where(dd["incl"], g_col - g_row, NEG_BIG))
        low.append(jnp.where(dd["strict"], b_col * kk[i] * decay, 0.0))
        qkd.append((qk[i] * decay).astype(BF16))
        e_col = jnp.exp(g_col)
        kf = kh[i].astype(F32)
        rhs.append(jnp.concatenate([vh[i].astype(F32) * b_col, kf * (b_col * e_col)], axis=1).astype(BF16))
        q_dec.append((qh[i].astype(F32) * e_col).astype(BF16))
        gl_col = jnp.concatenate(
            [jnp.broadcast_to(g_col[r:r + 1], (DN_CHUNK, 1)) for r in dd["last_row"]], axis=0)
        k_dec.append(kf * jnp.exp(gl_col - g_col))
        g_cols.append(g_col)

    def compact(x):
        return sum(x[c * DN_CHUNK:(c + 1) * DN_CHUNK] for c in range(nch))

    def expand(xc):
        return jnp.where(same, jnp.concatenate([xc] * nch, axis=0), 0.0).astype(BF16)

    eye_c = compact(eye)
    low_c = [compact(x) for x in low]
    xc = [_dot(low_c[i].astype(BF16), low[i].astype(BF16)) for i in range(nc)]
    rc = [eye_c - low_c[i] for i in range(nc)]
    n_sq = int(math.log2(DN_CHUNK)) - 1
    for it in range(n_sq):
        xbd = [expand(x) for x in xc]
        if it < n_sq - 1:
            both = [_dot(jnp.concatenate([rc[i], xc[i]], axis=0).astype(BF16), xbd[i]) for i in range(nc)]
            rc = [rc[i] + both[i][:DN_CHUNK] for i in range(nc)]
            xc = [both[i][DN_CHUNK:] for i in range(nc)]
        else:
            rc = [rc[i] + _dot(rc[i].astype(BF16), xbd[i]) for i in range(nc)]

    uw = [_dot(expand(rc[i]), rhs[i]) for i in range(nc)]
    u = [x[:, :DN_HEAD_DIM] for x in uw]
    w = [x[:, DN_HEAD_DIM:].astype(BF16) for x in uw]

    npair = nc // 2
    dk2 = 2 * DN_HEAD_DIM
    left = lax.broadcasted_iota(jnp.int32, (DN_HEAD_DIM, dk2), 1) < DN_HEAD_DIM
    state = [s_ref[pi] for pi in range(npair)]
    v_new = [[None] * nch for _ in range(nc)]
    o_q = [[None] * nch for _ in range(nc)]
    zeros_v = jnp.zeros((DN_CHUNK, DN_HEAD_DIM), BF16)
    chunk_of = lambda pi, step: dirs[chains[2 * pi][0]]["order"][step]
    kd_pair_t = [[jnp.concatenate([k_dec[2 * pi][c * DN_CHUNK:(c + 1) * DN_CHUNK],
                                   k_dec[2 * pi + 1][c * DN_CHUNK:(c + 1) * DN_CHUNK]], axis=0).T.astype(BF16)
                  for c in range(nch)] for pi in range(npair)]
    for step in range(nch):
        res = []
        for pi in range(npair):
            a, b = 2 * pi, 2 * pi + 1
            r0 = chunk_of(pi, step) * DN_CHUNK
            s_bd = jnp.concatenate([jnp.where(left, state[pi], 0.0), jnp.where(left, 0.0, state[pi])],
                                   axis=0).astype(BF16)
            lhs = jnp.concatenate(
                [jnp.concatenate([w[i][r0:r0 + DN_CHUNK], q_dec[i][r0:r0 + DN_CHUNK]], axis=0) for i in (a, b)],
                axis=1)
            res.append(_dot(lhs, s_bd))
        for pi in range(npair):
            c = chunk_of(pi, step)
            r0 = c * DN_CHUNK
            for side, i in enumerate((2 * pi, 2 * pi + 1)):
                part = res[pi][:, side * DN_HEAD_DIM:(side + 1) * DN_HEAD_DIM]
                v_new[i][c] = u[i][r0:r0 + DN_CHUNK] - part[:DN_CHUNK]
                o_q[i][c] = part[DN_CHUNK:]
        for pi in range(npair):
            a, b = 2 * pi, 2 * pi + 1
            c = chunk_of(pi, step)
            r = dirs[chains[a][0]]["last_row"][c]
            vn_bd = jnp.concatenate(
                [jnp.concatenate([v_new[a][c].astype(BF16), zeros_v], axis=1),
                 jnp.concatenate([zeros_v, v_new[b][c].astype(BF16)], axis=1)], axis=0)
            keep = jnp.where(left, jnp.exp(g_cols[a][r:r + 1]), jnp.exp(g_cols[b][r:r + 1]))
            state[pi] = state[pi] * keep + _dot(kd_pair_t[pi][c], vn_bd)
    for pi in range(npair):
        s_ref[pi] = state[pi]

    for i, (d, h) in enumerate(chains):
        lo = h * DN_HEAD_DIM
        vn_all = jnp.concatenate(v_new[i], axis=0).astype(BF16)
        o = jnp.concatenate(o_q[i], axis=0) + _dot(qkd[i], vn_all)
        refs[d][5][:, lo:lo + DN_HEAD_DIM] = o.astype(BF16)


def _delta_call(dq, dk, dv, bg, bgt, seq_len):
    n = dq.shape[0]
    batch = n // seq_len
    nb = seq_len // DN_BLK
    groups = batch // DN_SEQS
    dq, dk, dv = (a.reshape(groups, DN_SEQS, seq_len, DN_W) for a in (dq, dk, dv))
    bg = bg.reshape(groups, DN_SEQS, seq_len, LANES)
    bgt = bgt.reshape(groups, DN_SEQS, 4 * N_DN_HEADS, seq_len)
    fwd = lambda b, i: (b, 0, i, 0)
    bwd = lambda b, i: (b, 0, nb - 1 - i, 0)
    fwd_t = lambda b, i: (b, 0, 0, i)
    bwd_t = lambda b, i: (b, 0, 0, nb - 1 - i)
    tok = lambda m: pl.BlockSpec((1, DN_SEQS, DN_BLK, DN_W), m)
    in_specs = [tok(fwd), tok(fwd), tok(fwd), pl.BlockSpec((1, DN_SEQS, DN_BLK, LANES), fwd),
                pl.BlockSpec((1, DN_SEQS, 4 * N_DN_HEADS, DN_BLK), fwd_t),
                tok(bwd), tok(bwd), tok(bwd), pl.BlockSpec((1, DN_SEQS, DN_BLK, LANES), bwd),
                pl.BlockSpec((1, DN_SEQS, 4 * N_DN_HEADS, DN_BLK), bwd_t)]
    o_shape = jax.ShapeDtypeStruct((groups, DN_SEQS, seq_len, DN_W), BF16)
    o_f, o_b = pl.pallas_call(
        _delta_kernel,
        grid=(groups, nb),
        in_specs=in_specs,
        out_specs=(tok(fwd), tok(bwd)),
        out_shape=(o_shape, o_shape),
        scratch_shapes=[pltpu.VMEM((DN_SEQS * DN_DIRS * N_DN_HEADS // 2, DN_HEAD_DIM, 2 * DN_HEAD_DIM), F32)],
        compiler_params=pltpu.CompilerParams(dimension_semantics=("arbitrary", "arbitrary"),
                                             vmem_limit_bytes=VMEM_LIMIT_BYTES),
        name="delta",
    )(dq, dk, dv, bg, bgt, dq, dk, dv, bg, bgt)
    return o_f.reshape(n, DN_W), o_b.reshape(n, DN_W)


def _out_kernel(x_ref, attn_ref, of_ref, ob_ref, sz_ref, gate_ref, dng_ref, wa_ref, wb_ref, wo_ref,
                g1p_ref, g2_ref, w1_ref, w2_ref, g2p_ref, y_ref):
    o = of_ref[...].astype(F32) + ob_ref[...].astype(F32)
    dng = dng_ref[...]
    dn = jnp.concatenate([_rms(o[:, h * DN_HEAD_DIM:(h + 1) * DN_HEAD_DIM], dng)
                          for h in range(N_DN_HEADS)], axis=1)
    dn = (dn * sz_ref[...].astype(F32)).astype(BF16)
    a = _dot(attn_ref[...], wa_ref[...])
    dd = _dot(dn, wb_ref[...])
    gate = gate_ref[...].astype(F32)
    merged = (gate[:, :D_MODEL] * a + gate[:, D_MODEL:] * dd).astype(BF16)
    h1 = x_ref[...] + _rms(_dot(merged, wo_ref[...]), g1p_ref[...])
    hid = _dot(_rms(h1, g2_ref[...]).astype(BF16), w1_ref[...])
    hid = jnp.square(jnp.maximum(hid, 0.0)).astype(BF16)
    y_ref[...] = h1 + _rms(_dot(hid, w2_ref[...]), g2p_ref[...])


def _out_call(x2d, attn2d, o_f, o_b, sz, gate, dng, wa, wb, wo, g1p, g2, w1, w2, g2p):
    n = x2d.shape[0]
    tm = OUT_TM
    tok = lambda w: pl.BlockSpec((tm, w), lambda i: (i, 0))
    full = lambda a: pl.BlockSpec(a.shape, lambda i: (0, 0), pipeline_mode=pl.Buffered(1))
    return pl.pallas_call(
        _out_kernel,
        grid=(n // tm,),
        in_specs=[tok(D_MODEL), tok(ATT_Q), tok(DN_W), tok(DN_W), tok(DN_W), tok(2 * D_MODEL),
                  full(dng), full(wa), full(wb), full(wo), full(g1p), full(g2), full(w1), full(w2), full(g2p)],
        out_specs=tok(D_MODEL),
        out_shape=jax.ShapeDtypeStruct((n, D_MODEL), F32),
        compiler_params=pltpu.CompilerParams(dimension_semantics=("arbitrary",),
                                             vmem_limit_bytes=VMEM_LIMIT_BYTES),
        name="outmlp",
    )(x2d, attn2d, o_f, o_b, sz, gate, dng, wa, wb, wo, g1p, g2, w1, w2, g2p)


def _rope_table_t(seq_len):
    pos = jnp.arange(seq_len, dtype=jnp.int32)
    row_ids = (pos // GRID_W).astype(F32)
    col_ids = (pos % GRID_W).astype(F32)
    inv_freq = ROPE_THETA ** (-jnp.arange(0, ROPE_HALF, 2, dtype=F32) / ROPE_HALF)
    ang_r = inv_freq[:, None] * row_ids[None, :]
    ang_c = inv_freq[:, None] * col_ids[None, :]
    return jnp.concatenate([jnp.cos(ang_r), jnp.sin(ang_r), jnp.cos(ang_c), jnp.sin(ang_c)], axis=0)


def _pack_w_in(w_in):
    aq, ak, av, dqkv, dz, dbeta, da, gates = jnp.split(
        w_in, [ATT_Q, ATT_Q + ATT_KV, ATT_Q + 2 * ATT_KV, C_Z, C_Z + DN_W,
               C_Z + DN_W + 2 * N_DN_HEADS, C_Z + DN_W + 4 * N_DN_HEADS], axis=-1)
    pad = jnp.zeros((D_MODEL, LANES - 4 * N_DN_HEADS), w_in.dtype)
    return jnp.concatenate([aq, ak, av, dqkv, dz, gates, dbeta, da, pad], axis=-1).astype(BF16)


def _lane_row(v):
    v = v.reshape(-1).astype(F32)
    return jnp.zeros((1, LANES), F32).at[0, 2 * N_DN_HEADS:4 * N_DN_HEADS].set(v)


def _layer(x, p):
    batch, seq_len, _ = x.shape
    n = batch * seq_len
    x2d = x.reshape(n, D_MODEL)
    rope_t = _rope_table_t(seq_len)
    qt, k, vt, dq, dk, dv, sz, gate, bg, bgt = _proj_call(
        x2d, seq_len, p["g1"], p["w_pack"], rope_t, p["qg_t"], p["kg_t"], p["conv_w"], p["alog"], p["dtb"])
    attn = _attn_call(qt, k, vt).reshape(n, ATT_Q)
    o_f, o_b = _delta_call(dq, dk, dv, bg, bgt, seq_len)
    y = _out_call(x2d, attn, o_f, o_b, sz, gate, p["dng"], p["wa"], p["wb"], p["wo"],
                  p["g1p"], p["g2"], p["w1"], p["w2"], p["g2p"])
    return y.reshape(batch, seq_len, D_MODEL)


def kernel(x_prompt, x_sample, ln1_pre_g, w_in, attn_q_norm_g, attn_k_norm_g, dn_conv_w, dn_A_log, dn_dt_bias, dn_out_norm_g, w_attn_branch, w_dn_branch, w_out, ln1_post_g, ln2_pre_g, w_ff_in, w_ff_out, ln2_post_g):
    depth = w_in.shape[0]
    outs = []
    for x in (x_prompt, x_sample):
        for l in range(depth):
            p = {
                "g1": ln1_pre_g[l].reshape(1, D_MODEL),
                "w_pack": _pack_w_in(w_in[l]),
                "qg_t": jnp.broadcast_to(attn_q_norm_g[l][:, None], (HEAD_DIM, PROJ_TM)),
                "kg_t": jnp.broadcast_to(attn_k_norm_g[l][:, None], (HEAD_DIM, PROJ_TM)),
                "conv_w": dn_conv_w[l],
                "alog": _lane_row(dn_A_log[l]),
                "dtb": _lane_row(dn_dt_bias[l]),
                "dng": dn_out_norm_g[l].reshape(1, DN_HEAD_DIM),
                "wa": w_attn_branch[l].astype(BF16),
                "wb": w_dn_branch[l].astype(BF16),
                "wo": w_out[l].astype(BF16),
                "g1p": ln1_post_g[l].reshape(1, D_MODEL),
                "g2": ln2_pre_g[l].reshape(1, D_MODEL),
                "w1": w_ff_in[l].astype(BF16),
                "w2": w_ff_out[l].astype(BF16),
                "g2p": ln2_post_g[l].reshape(1, D_MODEL),
            }
            x = _layer(x, p)
        outs.append(x)
    return tuple(outs)
```

```python
import functools
import math

import jax
import jax.numpy as jnp
from jax import lax
from jax.experimental import pallas as pl
from jax.experimental.pallas import tpu as pltpu

D_MODEL = 1024
GRID_W = 64
N_Q_HEADS = 8
N_KV_HEADS = 2
HEAD_DIM = 64
GQA_GROUP = N_Q_HEADS // N_KV_HEADS
ROPE_HALF = HEAD_DIM // 2
ROPE_FREQS = ROPE_HALF // 2
ROPE_THETA = 10000.0
N_DN_HEADS = 4
DN_HEAD_DIM = 128
DN_CHUNK = 64
D_FF = 4 * D_MODEL
EPS = 1e-6
ATT_Q = N_Q_HEADS * HEAD_DIM
ATT_KV = N_KV_HEADS * HEAD_DIM
DN_W = N_DN_HEADS * DN_HEAD_DIM

LANES = 128
SUBLANES = 8
VMEM_LIMIT_BYTES = 56 * 1024 * 1024

C_Q = 0
C_K = C_Q + ATT_Q
C_V = C_K + ATT_KV
C_D = C_V + ATT_KV
C_Z = C_D + 3 * DN_W
C_G = C_Z + DN_W
C_B = C_G + 2 * D_MODEL
PACK_W = C_B + LANES

PROJ_TM = 512
OUT_TM = 512
ATT_TQ = 512
ATT_KVT_MAX = 1024
ATT_MIN_TILES = 8
ATT_VT = 256
ATT_SLOTS = 2
DN_BLK = 4 * DN_CHUNK
DN_DIRS = 2
DN_SEQS = 2

F32 = jnp.float32
BF16 = jnp.bfloat16
NEG_BIG = -1e30


def _rms(x, g):
    ms = jnp.mean(x * x, axis=-1, keepdims=True)
    return x * lax.rsqrt(ms + EPS) * g


def _sigmoid(x):
    return 1.0 / (1.0 + jnp.exp2(x * -math.log2(math.e)))


def _dot(a, b):
    return jnp.dot(a, b, preferred_element_type=F32)


def _dot_nt(a, b):
    return lax.dot_general(a, b, (((1,), (1,)), ((), ())), preferred_element_type=F32)


def _norm_rope_t(xt, gain_t, rope_t):
    ms = jnp.mean(xt * xt, axis=0, keepdims=True)
    xt = xt * lax.rsqrt(ms + EPS) * gain_t
    f = ROPE_FREQS
    x1r, x2r, x1c, x2c = xt[0:f], xt[f:2 * f], xt[2 * f:3 * f], xt[3 * f:4 * f]
    cr, sr, cc, sc = rope_t[0:f], rope_t[f:2 * f], rope_t[2 * f:3 * f], rope_t[3 * f:4 * f]
    return jnp.concatenate([x1r * cr - x2r * sr, x2r * cr + x1r * sr,
                            x1c * cc - x2c * sc, x2c * cc + x1c * sc], axis=0)


def _proj_kernel(x_ref, xp_ref, xn_ref, g1_ref, w_ref, rope_ref, qg_ref, kg_ref, conv_ref,
                 alog_ref, dtb_ref,
                 qt_ref, k_ref, vt_ref, dq_ref, dk_ref, dv_ref, sz_ref, gate_ref, bg_ref, bgt_ref,
                 *, tiles_per_seq):
    tm = x_ref.shape[0]
    ti = pl.program_id(0) % tiles_per_seq
    g1 = g1_ref[...]
    xb = _rms(x_ref[...], g1).astype(BF16)

    rope_t = rope_ref[...]
    yq_t = _dot(xb, w_ref[:, C_Q:C_Q + ATT_Q]).T
    q_scale = (HEAD_DIM ** -0.5) * math.log2(math.e)
    qg = qg_ref[...]
    for h in range(N_Q_HEADS):
        qh = _norm_rope_t(yq_t[h * HEAD_DIM:(h + 1) * HEAD_DIM], qg, rope_t)
        qt_ref[0, h] = (qh * q_scale).astype(BF16)

    yk_t = _dot(xb, w_ref[:, C_K:C_K + ATT_KV]).T
    kg = kg_ref[...]
    k_t = jnp.concatenate([_norm_rope_t(yk_t[h * HEAD_DIM:(h + 1) * HEAD_DIM], kg, rope_t)
                           for h in range(N_KV_HEADS)], axis=0)
    k_ref[0] = k_t.T.astype(BF16)
    yv_t = _dot(xb, w_ref[:, C_V:C_V + ATT_KV]).T.astype(BF16)
    for c in range(tm // ATT_VT):
        vt_ref[0, c] = yv_t[:, c * ATT_VT:(c + 1) * ATT_VT]

    halo = _rms(jnp.concatenate([xn_ref[...], xp_ref[...]], axis=0), g1).astype(BF16)
    y_ext = _dot(jnp.concatenate([xb, halo], axis=0), w_ref[:, C_D:C_D + 3 * DN_W])
    y = y_ext[0:tm]
    y_nxt = jnp.where(ti == tiles_per_seq - 1, 0.0, y_ext[tm:tm + SUBLANES])
    y_prv = jnp.where(ti == 0, 0.0, y_ext[tm + SUBLANES:])
    y_ext = jnp.concatenate([y, y_nxt, y_prv], axis=0)
    rows = tm + 2 * SUBLANES
    cw = conv_ref[...]
    c = (cw[0:1] * pltpu.roll(y_ext, 1, axis=0)[0:tm] + cw[1:2] * y
         + cw[2:3] * pltpu.roll(y_ext, rows - 1, axis=0)[0:tm])
    s = c * _sigmoid(c)
    for part, ref, scale in ((0, dq_ref, DN_HEAD_DIM ** -0.5), (1, dk_ref, 1.0)):
        outs = []
        for h in range(N_DN_HEADS):
            lo = part * DN_W + h * DN_HEAD_DIM
            xh = s[:, lo:lo + DN_HEAD_DIM]
            ss = jnp.sum(xh * xh, axis=-1, keepdims=True)
            outs.append(xh * (lax.rsqrt(ss + EPS) * scale))
        ref[...] = jnp.concatenate(outs, axis=1).astype(BF16)
    dv_ref[...] = s[:, 2 * DN_W:3 * DN_W].astype(BF16)

    z = _dot(xb, w_ref[:, C_Z:C_Z + DN_W])
    sz_ref[...] = (z * _sigmoid(z)).astype(BF16)
    gate_ref[...] = _sigmoid(_dot(xb, w_ref[:, C_G:C_G + 2 * D_MODEL])).astype(BF16)

    yb = _dot(xb, w_ref[:, C_B:C_B + LANES])
    lane = lax.broadcasted_iota(jnp.int32, (1, LANES), 1)
    t = yb + dtb_ref[...]
    softplus = jnp.maximum(t, 0.0) + jnp.log1p(jnp.exp(-jnp.abs(t)))
    bg = jnp.where(lane < 2 * N_DN_HEADS, _sigmoid(yb), -jnp.exp(alog_ref[...]) * softplus)
    bg = jnp.where(lane < 4 * N_DN_HEADS, bg, 0.0)
    bg_ref[...] = bg
    bgt_ref[0] = bg.T[0:4 * N_DN_HEADS]


def _proj_call(x2d, seq_len, g1, w_pack, rope_t, qg_t, kg_t, conv_w, alog_row, dtb_row):
    n = x2d.shape[0]
    tm = PROJ_TM
    batch = n // seq_len
    tps = seq_len // tm
    hb = tm // SUBLANES
    n8 = n // SUBLANES
    const = lambda i: (0, 0)
    tok = lambda i: (i, 0)
    in_specs = [
        pl.BlockSpec((tm, D_MODEL), tok),
        pl.BlockSpec((SUBLANES, D_MODEL), lambda i: (jnp.maximum(i * hb - 1, 0), 0)),
        pl.BlockSpec((SUBLANES, D_MODEL), lambda i: (jnp.minimum((i + 1) * hb, n8 - 1), 0)),
        pl.BlockSpec((1, D_MODEL), const),
        pl.BlockSpec((D_MODEL, PACK_W), const, pipeline_mode=pl.Buffered(1)),
        pl.BlockSpec((HEAD_DIM, tm), lambda i: (0, i % tps)),
        pl.BlockSpec((HEAD_DIM, tm), const),
        pl.BlockSpec((HEAD_DIM, tm), const),
        pl.BlockSpec((3, 3 * DN_W), const),
        pl.BlockSpec((1, LANES), const),
        pl.BlockSpec((1, LANES), const),
    ]
    out_shape = (
        jax.ShapeDtypeStruct((batch, N_Q_HEADS, HEAD_DIM, seq_len), BF16),
        jax.ShapeDtypeStruct((batch, seq_len, ATT_KV), BF16),
        jax.ShapeDtypeStruct((batch, seq_len // ATT_VT, ATT_KV, ATT_VT), BF16),
        jax.ShapeDtypeStruct((n, DN_W), BF16),
        jax.ShapeDtypeStruct((n, DN_W), BF16),
        jax.ShapeDtypeStruct((n, DN_W), BF16),
        jax.ShapeDtypeStruct((n, DN_W), BF16),
        jax.ShapeDtypeStruct((n, 2 * D_MODEL), BF16),
        jax.ShapeDtypeStruct((n, LANES), F32),
        jax.ShapeDtypeStruct((batch, 4 * N_DN_HEADS, seq_len), F32),
    )
    kvc = tm // ATT_VT
    out_specs = (
        pl.BlockSpec((1, N_Q_HEADS, HEAD_DIM, tm), lambda i: (i // tps, 0, 0, i % tps)),
        pl.BlockSpec((1, tm, ATT_KV), lambda i: (i // tps, i % tps, 0)),
        pl.BlockSpec((1, kvc, ATT_KV, ATT_VT), lambda i: (i // tps, i % tps, 0, 0)),
        pl.BlockSpec((tm, DN_W), tok),
        pl.BlockSpec((tm, DN_W), tok),
        pl.BlockSpec((tm, DN_W), tok),
        pl.BlockSpec((tm, DN_W), tok),
        pl.BlockSpec((tm, 2 * D_MODEL), tok),
        pl.BlockSpec((tm, LANES), tok),
        pl.BlockSpec((1, 4 * N_DN_HEADS, tm), lambda i: (i // tps, 0, i % tps)),
    )
    return pl.pallas_call(
        functools.partial(_proj_kernel, tiles_per_seq=tps),
        grid=(n // tm,),
        in_specs=in_specs,
        out_specs=out_specs,
        out_shape=out_shape,
        compiler_params=pltpu.CompilerParams(dimension_semantics=("arbitrary",),
                                             vmem_limit_bytes=VMEM_LIMIT_BYTES),
        name="proj",
    )(x2d, x2d, x2d, g1, w_pack, rope_t, qg_t, kg_t, conv_w, alog_row, dtb_row)


def _attn_kernel(q_ref, k_ref, v_ref, o_ref, s_scr, acc_scr, *, n_kv_tiles):
    kvh = pl.program_id(1)
    tq = q_ref.shape[3]
    nq = GQA_GROUP * tq
    q4 = q_ref[0]
    qcat = jnp.concatenate([q4[g] for g in range(GQA_GROUP)], axis=1).astype(F32)
    zero = jnp.zeros_like(qcat)
    qpad = jnp.where(kvh == 0, jnp.concatenate([qcat, zero], axis=0),
                     jnp.concatenate([zero, qcat], axis=0)).astype(BF16)
    kvt = s_scr.shape[1]
    vt_per_tile = kvt // ATT_VT

    def scores(slot, j):
        kt = k_ref[0, pl.ds(pl.multiple_of(j * kvt, kvt), kvt), :]
        s = _dot(kt, qpad)
        s_scr[slot] = s
        return jnp.max(s, axis=0, keepdims=True)

    def update(slot, j, m, l, tile_max):
        m_new = jnp.maximum(m, tile_max)
        alpha = jnp.exp2(m - m_new)
        p = jnp.exp2(s_scr[slot] - m_new)
        l = alpha * l + jnp.sum(p, axis=0, keepdims=True)
        vt = jnp.concatenate([v_ref[0, j * vt_per_tile + c] for c in range(vt_per_tile)], axis=1)
        acc_scr[...] = alpha * acc_scr[...] + _dot(vt, p.astype(BF16))
        return m_new, l

    acc_scr[...] = jnp.zeros_like(acc_scr)
    ahead = ATT_SLOTS // 2
    maxes = tuple(scores(t, t) for t in range(ahead))

    def group(j0, m, l, maxes, produce_next):
        maxes = list(maxes) + [None] * (ATT_SLOTS - ahead)
        for t in range(ATT_SLOTS):
            tp = t + ahead
            if tp < ATT_SLOTS or produce_next:
                maxes[tp % ATT_SLOTS] = scores(tp % ATT_SLOTS, j0 + tp)
            m, l = update(t, j0 + t, m, l, maxes[t])
        return m, l, tuple(maxes[:ahead])

    init = (jnp.full((1, nq), NEG_BIG, F32), jnp.zeros((1, nq), F32), maxes)
    n_groups = n_kv_tiles // ATT_SLOTS
    m, l, maxes = lax.fori_loop(0, n_groups - 1, lambda jj, c: group(ATT_SLOTS * jj, *c, True), init)
    _, l, _ = group(ATT_SLOTS * (n_groups - 1), m, l, maxes, False)
    out = acc_scr[...] * (1.0 / l)
    out = jnp.concatenate([out[:, g * tq:(g + 1) * tq] for g in range(GQA_GROUP)], axis=0)
    o_ref[0] = out.T.astype(BF16)


def _attn_call(qt, k, vt):
    batch, _, _, seq_len = qt.shape
    tq = ATT_TQ
    kvt = min(ATT_KVT_MAX, seq_len // ATT_MIN_TILES)
    nkv = seq_len // kvt
    assert nkv % ATT_SLOTS == 0 and kvt % ATT_VT == 0
    nq = GQA_GROUP * tq
    return pl.pallas_call(
        functools.partial(_attn_kernel, n_kv_tiles=nkv),
        grid=(batch, N_KV_HEADS, seq_len // tq),
        in_specs=[
            pl.BlockSpec((1, GQA_GROUP, HEAD_DIM, tq), lambda b, h, i: (b, h, 0, i)),
            pl.BlockSpec((1, seq_len, ATT_KV), lambda b, h, i: (b, 0, 0)),
            pl.BlockSpec((1, seq_len // ATT_VT, HEAD_DIM, ATT_VT), lambda b, h, i: (b, 0, h, 0)),
        ],
        out_specs=pl.BlockSpec((1, tq, GQA_GROUP * HEAD_DIM), lambda b, h, i: (b, i, h)),
        out_shape=jax.ShapeDtypeStruct((batch, seq_len, ATT_Q), BF16),
        scratch_shapes=[pltpu.VMEM((ATT_SLOTS, kvt, nq), F32), pltpu.VMEM((HEAD_DIM, nq), F32)],
        compiler_params=pltpu.CompilerParams(
            dimension_semantics=("arbitrary", "arbitrary", "arbitrary"),
            vmem_limit_bytes=VMEM_LIMIT_BYTES),
        name="attn",
    )(qt, k, vt)


def _split3(x):
    p1 = x.astype(BF16)
    r1 = x - p1.astype(F32)
    p2 = r1.astype(BF16)
    p3 = (r1 - p2.astype(F32)).astype(BF16)
    return p1, p2, p3


def _delta_kernel(qf_ref, kf_ref, vf_ref, bgf_ref, bgtf_ref, qb_ref, kb_ref, vb_ref, bgb_ref, bgtb_ref,
                  of_ref, ob_ref, s_ref):
    @pl.when(pl.program_id(1) == 0)
    def _():
        s_ref[...] = jnp.zeros_like(s_ref)

    blk = DN_BLK
    nch = blk // DN_CHUNK
    ri = lax.broadcasted_iota(jnp.int32, (blk, blk), 0)
    ci = lax.broadcasted_iota(jnp.int32, (blk, blk), 1)
    same = (ri // DN_CHUNK) == (ci // DN_CHUNK)
    lower_incl, lower_strict = same & (ri >= ci), same & (ri > ci)
    upper_incl, upper_strict = same & (ri <= ci), same & (ri < ci)
    eye = (ri == ci).astype(F32)
    dir_refs = ((qf_ref, kf_ref, vf_ref, bgf_ref, bgtf_ref, of_ref),
                (qb_ref, kb_ref, vb_ref, bgb_ref, bgtb_ref, ob_ref))
    n_streams = DN_SEQS * DN_DIRS
    refs = [tuple(r.at[0, d // DN_DIRS] for r in dir_refs[d % DN_DIRS]) for d in range(n_streams)]

    dirs = []
    for d in range(n_streams):
        reverse = d % DN_DIRS == 1
        incl, strict, incl_t = ((upper_incl, upper_strict, lower_incl) if reverse
                                else (lower_incl, lower_strict, upper_incl))
        bg = refs[d][3][...]
        bgt = refs[d][4][...]
        m_col = jnp.where(incl, 1.0, 0.0).astype(BF16)
        m_row = jnp.where(incl_t, 1.0, 0.0).astype(BF16)
        gc = sum(_dot(m_col, piece) for piece in _split3(bg))
        gct = sum(_dot(piece, m_row) for piece in _split3(bgt))
        last_row = [(c * DN_CHUNK if reverse else (c + 1) * DN_CHUNK - 1) for c in range(nch)]
        order = list(range(nch - 1, -1, -1)) if reverse else list(range(nch))
        dirs.append(dict(incl=incl, strict=strict, bg=bg, gc=gc, gct=gct, last_row=last_row, order=order))

    chains = [(d, h) for d in range(n_streams) for h in range(N_DN_HEADS)]
    nc = len(chains)

    qh, kh, vh, kk, qk = [], [], [], [], []
    for d, h in chains:
        lo = h * DN_HEAD_DIM
        qh.append(refs[d][0][:, lo:lo + DN_HEAD_DIM])
        kh.append(refs[d][1][:, lo:lo + DN_HEAD_DIM])
        vh.append(refs[d][2][:, lo:lo + DN_HEAD_DIM])
    for i in range(nc):
        kk.append(_dot_nt(kh[i], kh[i]))
        qk.append(_dot_nt(qh[i], kh[i]))

    low, qkd, rhs, q_dec, k_dec, g_cols = [], [], [], [], [], []
    for i, (d, h) in enumerate(chains):
        dd = dirs[d]
        cb = (d % DN_DIRS) * N_DN_HEADS + h
        cg = 2 * N_DN_HEADS + cb
        b_col = dd["bg"][:, cb:cb + 1]
        g_col = dd["gc"][:, cg:cg + 1]
        g_row = dd["gct"][cg:cg + 1, :]
        decay = jnp.exp(jnp.where(dd["incl"], g_col - g_row, NEG_BIG))
        low.append(jnp.where(dd["strict"], b_col * kk[i] * decay, 0.0))
        qkd.append((qk[i] * decay).astype(BF16))
        e_col = jnp.exp(g_col)
        kf = kh[i].astype(F32)
        rhs.append(jnp.concatenate([vh[i].astype(F32) * b_col, kf * (b_col * e_col)], axis=1).astype(BF16))
        q_dec.append((qh[i].astype(F32) * e_col).astype(BF16))
        gl_col = jnp.concatenate(
            [jnp.broadcast_to(g_col[r:r + 1], (DN_CHUNK, 1)) for r in dd["last_row"]], axis=0)
        k_dec.append(kf * jnp.exp(gl_col - g_col))
        g_cols.append(g_col)

    def compact(x):
        return sum(x[c * DN_CHUNK:(c + 1) * DN_CHUNK] for c in range(nch))

    def expand(xc):
        return jnp.where(same, jnp.concatenate([xc] * nch, axis=0), 0.0).astype(BF16)

    eye_c = compact(eye)
    low_c = [compact(x) for x in low]
    xc = [_dot(low_c[i].astype(BF16), low[i].astype(BF16)) for i in range(nc)]
    rc = [eye_c - low_c[i] for i in range(nc)]
    n_sq = int(math.log2(DN_CHUNK)) - 1
    for it in range(n_sq):
        xbd = [expand(x) for x in xc]
        if it < n_sq - 1:
            both = [_dot(jnp.concatenate([rc[i], xc[i]], axis=0).astype(BF16), xbd[i]) for i in range(nc)]
            rc = [rc[i] + both[i][:DN_CHUNK] for i in range(nc)]
            xc = [both[i][DN_CHUNK:] for i in range(nc)]
        else:
            rc = [rc[i] + _dot(rc[i].astype(BF16), xbd[i]) for i in range(nc)]

    uw = [_dot(expand(rc[i]), rhs[i]) for i in range(nc)]
    u = [x[:, :DN_HEAD_DIM] for x in uw]
    w = [x[:, DN_HEAD_DIM:].astype(BF16) for x in uw]

    npair = nc // 2
    dk2 = 2 * DN_HEAD_DIM
    left = lax.broadcasted_iota(jnp.int32, (DN_HEAD_DIM, dk2), 1) < DN_HEAD_DIM
    state = [s_ref[pi] for pi in range(npair)]
    v_new = [[None] * nch for _ in range(nc)]
    o_q = [[None] * nch for _ in range(nc)]
    zeros_v = jnp.zeros((DN_CHUNK, DN_HEAD_DIM), BF16)
    chunk_of = lambda pi, step: dirs[chains[2 * pi][0]]["order"][step]
    kd_pair_t = [[jnp.concatenate([k_dec[2 * pi][c * DN_CHUNK:(c + 1) * DN_CHUNK],
                                   k_dec[2 * pi + 1][c * DN_CHUNK:(c + 1) * DN_CHUNK]], axis=0).T.astype(BF16)
                  for c in range(nch)] for pi in range(npair)]
    for step in range(nch):
        res = []
        for pi in range(npair):
            a, b = 2 * pi, 2 * pi + 1
            r0 = chunk_of(pi, step) * DN_CHUNK
            s_bd = jnp.concatenate([jnp.where(left, state[pi], 0.0), jnp.where(left, 0.0, state[pi])],
                                   axis=0).astype(BF16)
            lhs = jnp.concatenate(
                [jnp.concatenate([w[i][r0:r0 + DN_CHUNK], q_dec[i][r0:r0 + DN_CHUNK]], axis=0) for i in (a, b)],
                axis=1)
            res.append(_dot(lhs, s_bd))
        for pi in range(npair):
            c = chunk_of(pi, step)
            r0 = c * DN_CHUNK
            for side, i in enumerate((2 * pi, 2 * pi + 1)):
                part = res[pi][:, side * DN_HEAD_DIM:(side + 1) * DN_HEAD_DIM]
                v_new[i][c] = u[i][r0:r0 + DN_CHUNK] - part[:DN_CHUNK]
                o_q[i][c] = part[DN_CHUNK:]
        for pi in range(npair):
            a, b = 2 * pi, 2 * pi + 1
            c = chunk_of(pi, step)
            r = dirs[chains[a][0]]["last_row"][c]
            vn_bd = jnp.concatenate(
                [jnp.concatenate([v_new[a][c].astype(BF16), zeros_v], axis=1),
                 jnp.concatenate([zeros_v, v_new[b][c].astype(BF16)], axis=1)], axis=0)
            keep = jnp.where(left, jnp.exp(g_cols[a][r:r + 1]), jnp.exp(g_cols[b][r:r + 1]))
            state[pi] = state[pi] * keep + _dot(kd_pair_t[pi][c], vn_bd)
    for pi in range(npair):
        s_ref[pi] = state[pi]

    for i, (d, h) in enumerate(chains):
        lo = h * DN_HEAD_DIM
        vn_all = jnp.concatenate(v_new[i], axis=0).astype(BF16)
        o = jnp.concatenate(o_q[i], axis=0) + _dot(qkd[i], vn_all)
        refs[d][5][:, lo:lo + DN_HEAD_DIM] = o.astype(BF16)


def _delta_call(dq, dk, dv, bg, bgt, seq_len):
    n = dq.shape[0]
    batch = n // seq_len
    nb = seq_len // DN_BLK
    groups = batch // DN_SEQS
    dq, dk, dv = (a.reshape(groups, DN_SEQS, seq_len, DN_W) for a in (dq, dk, dv))
    bg = bg.reshape(groups, DN_SEQS, seq_len, LANES)
    bgt = bgt.reshape(groups, DN_SEQS, 4 * N_DN_HEADS, seq_len)
    fwd = lambda b, i: (b, 0, i, 0)
    bwd = lambda b, i: (b, 0, nb - 1 - i, 0)
    fwd_t = lambda b, i: (b, 0, 0, i)
    bwd_t = lambda b, i: (b, 0, 0, nb - 1 - i)
    tok = lambda m: pl.BlockSpec((1, DN_SEQS, DN_BLK, DN_W), m)
    in_specs = [tok(fwd), tok(fwd), tok(fwd), pl.BlockSpec((1, DN_SEQS, DN_BLK, LANES), fwd),
                pl.BlockSpec((1, DN_SEQS, 4 * N_DN_HEADS, DN_BLK), fwd_t),
                tok(bwd), tok(bwd), tok(bwd), pl.BlockSpec((1, DN_SEQS, DN_BLK, LANES), bwd),
                pl.BlockSpec((1, DN_SEQS, 4 * N_DN_HEADS, DN_BLK), bwd_t)]
    o_shape = jax.ShapeDtypeStruct((groups, DN_SEQS, seq_len, DN_W), BF16)
    o_f, o_b = pl.pallas_call(
        _delta_kernel,
        grid=(groups, nb),
        in_specs=in_specs,
        out_specs=(tok(fwd), tok(bwd)),
        out_shape=(o_shape, o_shape),
        scratch_shapes=[pltpu.VMEM((DN_SEQS * DN_DIRS * N_DN_HEADS // 2, DN_HEAD_DIM, 2 * DN_HEAD_DIM), F32)],
        compiler_params=pltpu.CompilerParams(dimension_semantics=("arbitrary", "arbitrary"),
                                             vmem_limit_bytes=VMEM_LIMIT_BYTES),
        name="delta",
    )(dq, dk, dv, bg, bgt, dq, dk, dv, bg, bgt)
    return o_f.reshape(n, DN_W), o_b.reshape(n, DN_W)


def _out_kernel(x_ref, attn_ref, of_ref, ob_ref, sz_ref, gate_ref, dng_ref, wa_ref, wb_ref, wo_ref,
                g1p_ref, g2_ref, w1_ref, w2_ref, g2p_ref, y_ref):
    o = of_ref[...].astype(F32) + ob_ref[...].astype(F32)
    dng = dng_ref[...]
    dn = jnp.concatenate([_rms(o[:, h * DN_HEAD_DIM:(h + 1) * DN_HEAD_DIM], dng)
                          for h in range(N_DN_HEADS)], axis=1)
    dn = (dn * sz_ref[...].astype(F32)).astype(BF16)
    a = _dot(attn_ref[...], wa_ref[...])
    dd = _dot(dn, wb_ref[...])
    gate = gate_ref[...].astype(F32)
    merged = (gate[:, :D_MODEL] * a + gate[:, D_MODEL:] * dd).astype(BF16)
    h1 = x_ref[...] + _rms(_dot(merged, wo_ref[...]), g1p_ref[...])
    hid = _dot(_rms(h1, g2_ref[...]).astype(BF16), w1_ref[...])
    hid = jnp.square(jnp.maximum(hid, 0.0)).astype(BF16)
    y_ref[...] = h1 + _rms(_dot(hid, w2_ref[...]), g2p_ref[...])


def _out_call(x2d, attn2d, o_f, o_b, sz, gate, dng, wa, wb, wo, g1p, g2, w1, w2, g2p):
    n = x2d.shape[0]
    tm = OUT_TM
    tok = lambda w: pl.BlockSpec((tm, w), lambda i: (i, 0))
    full = lambda a: pl.BlockSpec(a.shape, lambda i: (0, 0), pipeline_mode=pl.Buffered(1))
    return pl.pallas_call(
        _out_kernel,
        grid=(n // tm,),
        in_specs=[tok(D_MODEL), tok(ATT_Q), tok(DN_W), tok(DN_W), tok(DN_W), tok(2 * D_MODEL),
                  full(dng), full(wa), full(wb), full(wo), full(g1p), full(g2), full(w1), full(w2), full(g2p)],
        out_specs=tok(D_MODEL),
        out_shape=jax.ShapeDtypeStruct((n, D_MODEL), F32),
        compiler_params=pltpu.CompilerParams(dimension_semantics=("arbitrary",),
                                             vmem_limit_bytes=VMEM_LIMIT_BYTES),
        name="outmlp",
    )(x2d, attn2d, o_f, o_b, sz, gate, dng, wa, wb, wo, g1p, g2, w1, w2, g2p)


def _rope_table_t(seq_len):
    pos = jnp.arange(seq_len, dtype=jnp.int32)
    row_ids = (pos // GRID_W).astype(F32)
    col_ids = (pos % GRID_W).astype(F32)
    inv_freq = ROPE_THETA ** (-jnp.arange(0, ROPE_HALF, 2, dtype=F32) / ROPE_HALF)
    ang_r = inv_freq[:, None] * row_ids[None, :]
    ang_c = inv_freq[:, None] * col_ids[None, :]
    return jnp.concatenate([jnp.cos(ang_r), jnp.sin(ang_r), jnp.cos(ang_c), jnp.sin(ang_c)], axis=0)


def _pack_w_in(w_in):
    aq, ak, av, dqkv, dz, dbeta, da, gates = jnp.split(
        w_in, [ATT_Q, ATT_Q + ATT_KV, ATT_Q + 2 * ATT_KV, C_Z, C_Z + DN_W,
               C_Z + DN_W + 2 * N_DN_HEADS, C_Z + DN_W + 4 * N_DN_HEADS], axis=-1)
    pad = jnp.zeros((D_MODEL, LANES - 4 * N_DN_HEADS), w_in.dtype)
    return jnp.concatenate([aq, ak, av, dqkv, dz, gates, dbeta, da, pad], axis=-1).astype(BF16)


def _lane_row(v):
    v = v.reshape(-1).astype(F32)
    return jnp.zeros((1, LANES), F32).at[0, 2 * N_DN_HEADS:4 * N_DN_HEADS].set(v)


def _layer(x, p):
    batch, seq_len, _ = x.shape
    n = batch * seq_len
    x2d = x.reshape(n, D_MODEL)
    rope_t = _rope_table_t(seq_len)
    qt, k, vt, dq, dk, dv, sz, gate, bg, bgt = _proj_call(
        x2d, seq_len, p["g1"], p["w_pack"], rope_t, p["qg_t"], p["kg_t"], p["conv_w"], p["alog"], p["dtb"])
    attn = _attn_call(qt, k, vt).reshape(n, ATT_Q)
    o_f, o_b = _delta_call(dq, dk, dv, bg, bgt, seq_len)
    y = _out_call(x2d, attn, o_f, o_b, sz, gate, p["dng"], p["wa"], p["wb"], p["wo"],
                  p["g1p"], p["g2"], p["w1"], p["w2"], p["g2p"])
    return y.reshape(batch, seq_len, D_MODEL)


def kernel(x_prompt, x_sample, ln1_pre_g, w_in, attn_q_norm_g, attn_k_norm_g, dn_conv_w, dn_A_log, dn_dt_bias, dn_out_norm_g, w_attn_branch, w_dn_branch, w_out, ln1_post_g, ln2_pre_g, w_ff_in, w_ff_out, ln2_post_g):
    depth = w_in.shape[0]
    outs = []
    for x in (x_prompt, x_sample):
        for l in range(depth):
            p = {
                "g1": ln1_pre_g[l].reshape(1, D_MODEL),
                "w_pack": _pack_w_in(w_in[l]),
                "qg_t": jnp.broadcast_to(attn_q_norm_g[l][:, None], (HEAD_DIM, PROJ_TM)),
                "kg_t": jnp.broadcast_to(attn_k_norm_g[l][:, None], (HEAD_DIM, PROJ_TM)),
                "conv_w": dn_conv_w[l],
                "alog": _lane_row(dn_A_log[l]),
                "dtb": _lane_row(dn_dt_bias[l]),
                "dng": dn_out_norm_g[l].reshape(1, DN_HEAD_DIM),
                "wa": w_attn_branch[l].astype(BF16),
                "wb": w_dn_branch[l].astype(BF16),
                "wo": w_out[l].astype(BF16),
                "g1p": ln1_post_g[l].reshape(1, D_MODEL),
                "g2": ln2_pre_g[l].reshape(1, D_MODEL),
                "w1": w_ff_in[l].astype(BF16),
                "w2": w_ff_out[l].astype(BF16),
                "g2p": ln2_post_g[l].reshape(1, D_MODEL),
            }
            x = _layer(x, p)
        outs.append(x)
    return tuple(outs)
```

```python
import functools
import math

import jax
import jax.numpy as jnp
from jax import lax
from jax.experimental import pallas as pl
from jax.experimental.pallas import tpu as pltpu

D_MODEL = 1024
GRID_W = 64
N_Q_HEADS = 8
N_KV_HEADS = 2
HEAD_DIM = 64
GQA_GROUP = N_Q_HEADS // N_KV_HEADS
ROPE_HALF = HEAD_DIM // 2
ROPE_FREQS = ROPE_HALF // 2
ROPE_THETA = 10000.0
N_DN_HEADS = 4
DN_HEAD_DIM = 128
DN_CHUNK = 64
D_FF = 4 * D_MODEL
EPS = 1e-6
ATT_Q = N_Q_HEADS * HEAD_DIM
ATT_KV = N_KV_HEADS * HEAD_DIM
DN_W = N_DN_HEADS * DN_HEAD_DIM

LANES = 128
SUBLANES = 8
VMEM_LIMIT_BYTES = 56 * 1024 * 1024

C_Q = 0
C_K = C_Q + ATT_Q
C_V = C_K + ATT_KV
C_D = C_V + ATT_KV
C_Z = C_D + 3 * DN_W
C_G = C_Z + DN_W
C_B = C_G + 2 * D_MODEL
PACK_W = C_B + LANES

PROJ_TM = 512
OUT_TM = 512
ATT_TILE_SCORES = 1024 * 2048
ATT_KVT_MAX = 1024
ATT_MIN_TILES = 8
ATT_VT = 256
ATT_SLOTS = 2
DN_BLK = 4 * DN_CHUNK
DN_DIRS = 2
DN_SEQS = 2

F32 = jnp.float32
BF16 = jnp.bfloat16
NEG_BIG = -1e30


def _rms(x, g):
    ms = jnp.mean(x * x, axis=-1, keepdims=True)
    return x * lax.rsqrt(ms + EPS) * g


def _sigmoid(x):
    return 1.0 / (1.0 + jnp.exp2(x * -math.log2(math.e)))


def _dot(a, b):
    return jnp.dot(a, b, preferred_element_type=F32)


def _dot_nt(a, b):
    return lax.dot_general(a, b, (((1,), (1,)), ((), ())), preferred_element_type=F32)


def _norm_rope_t(xt, gain_t, rope_t):
    ms = jnp.mean(xt * xt, axis=0, keepdims=True)
    xt = xt * lax.rsqrt(ms + EPS) * gain_t
    f = ROPE_FREQS
    x1r, x2r, x1c, x2c = xt[0:f], xt[f:2 * f], xt[2 * f:3 * f], xt[3 * f:4 * f]
    cr, sr, cc, sc = rope_t[0:f], rope_t[f:2 * f], rope_t[2 * f:3 * f], rope_t[3 * f:4 * f]
    return jnp.concatenate([x1r * cr - x2r * sr, x2r * cr + x1r * sr,
                            x1c * cc - x2c * sc, x2c * cc + x1c * sc], axis=0)


def _proj_kernel(x_ref, xp_ref, xn_ref, g1_ref, w_ref, rope_ref, qg_ref, kg_ref, conv_ref,
                 alog_ref, dtb_ref,
                 qt_ref, k_ref, vt_ref, dq_ref, dk_ref, dv_ref, sz_ref, gate_ref, bg_ref, bgt_ref,
                 *, tiles_per_seq):
    tm = x_ref.shape[0]
    ti = pl.program_id(0) % tiles_per_seq
    g1 = g1_ref[...]
    xb = _rms(x_ref[...], g1).astype(BF16)

    halo = _rms(jnp.concatenate([xn_ref[...], xp_ref[...]], axis=0), g1).astype(BF16)
    y_ext = _dot(jnp.concatenate([xb, halo], axis=0), w_ref[:, C_D:C_D + 3 * DN_W])
    y = y_ext[0:tm]
    y_nxt = jnp.where(ti == tiles_per_seq - 1, 0.0, y_ext[tm:tm + SUBLANES])
    y_prv = jnp.where(ti == 0, 0.0, y_ext[tm + SUBLANES:])
    y_ext = jnp.concatenate([y, y_nxt, y_prv], axis=0)
    rows = tm + 2 * SUBLANES
    cw = conv_ref[...]
    c = (cw[0:1] * pltpu.roll(y_ext, 1, axis=0)[0:tm] + cw[1:2] * y
         + cw[2:3] * pltpu.roll(y_ext, rows - 1, axis=0)[0:tm])
    s = c * _sigmoid(c)
    for part, ref, scale in ((0, dq_ref, DN_HEAD_DIM ** -0.5), (1, dk_ref, 1.0)):
        outs = []
        for h in range(N_DN_HEADS):
            lo = part * DN_W + h * DN_HEAD_DIM
            xh = s[:, lo:lo + DN_HEAD_DIM]
            ss = jnp.sum(xh * xh, axis=-1, keepdims=True)
            outs.append(xh * (lax.rsqrt(ss + EPS) * scale))
        ref[...] = jnp.concatenate(outs, axis=1).astype(BF16)
    dv_ref[...] = s[:, 2 * DN_W:3 * DN_W].astype(BF16)

    z = _dot(xb, w_ref[:, C_Z:C_Z + DN_W])
    sz_ref[...] = (z * _sigmoid(z)).astype(BF16)
    gate_ref[...] = _sigmoid(_dot(xb, w_ref[:, C_G:C_G + 2 * D_MODEL])).astype(BF16)

    rope_t = rope_ref[...]
    yq_t = _dot(xb, w_ref[:, C_Q:C_Q + ATT_Q]).T
    q_scale = (HEAD_DIM ** -0.5) * math.log2(math.e)
    qg = qg_ref[...]
    for h in range(N_Q_HEADS):
        qh = _norm_rope_t(yq_t[h * HEAD_DIM:(h + 1) * HEAD_DIM], qg, rope_t)
        qt_ref[0, h] = (qh * q_scale).astype(BF16)

    yk_t = _dot(xb, w_ref[:, C_K:C_K + ATT_KV]).T
    kg = kg_ref[...]
    k_t = jnp.concatenate([_norm_rope_t(yk_t[h * HEAD_DIM:(h + 1) * HEAD_DIM], kg, rope_t)
                           for h in range(N_KV_HEADS)], axis=0)
    k_ref[0] = k_t.T.astype(BF16)
    yv_t = _dot(xb, w_ref[:, C_V:C_V + ATT_KV]).T.astype(BF16)
    for c in range(tm // ATT_VT):
        vt_ref[0, c] = yv_t[:, c * ATT_VT:(c + 1) * ATT_VT]

    yb = _dot(xb, w_ref[:, C_B:C_B + LANES])
    lane = lax.broadcasted_iota(jnp.int32, (1, LANES), 1)
    t = yb + dtb_ref[...]
    softplus = jnp.maximum(t, 0.0) + jnp.log1p(jnp.exp(-jnp.abs(t)))
    bg = jnp.where(lane < 2 * N_DN_HEADS, _sigmoid(yb), -jnp.exp(alog_ref[...]) * softplus)
    bg = jnp.where(lane < 4 * N_DN_HEADS, bg, 0.0)
    bg_ref[...] = bg
    bgt_ref[0] = bg.T[0:4 * N_DN_HEADS]


def _proj_call(x2d, seq_len, g1, w_pack, rope_t, qg_t, kg_t, conv_w, alog_row, dtb_row):
    n = x2d.shape[0]
    tm = PROJ_TM
    batch = n // seq_len
    tps = seq_len // tm
    hb = tm // SUBLANES
    n8 = n // SUBLANES
    const = lambda i: (0, 0)
    tok = lambda i: (i, 0)
    in_specs = [
        pl.BlockSpec((tm, D_MODEL), tok),
        pl.BlockSpec((SUBLANES, D_MODEL), lambda i: (jnp.maximum(i * hb - 1, 0), 0)),
        pl.BlockSpec((SUBLANES, D_MODEL), lambda i: (jnp.minimum((i + 1) * hb, n8 - 1), 0)),
        pl.BlockSpec((1, D_MODEL), const),
        pl.BlockSpec((D_MODEL, PACK_W), const, pipeline_mode=pl.Buffered(1)),
        pl.BlockSpec((HEAD_DIM, tm), lambda i: (0, i % tps)),
        pl.BlockSpec((HEAD_DIM, tm), const),
        pl.BlockSpec((HEAD_DIM, tm), const),
        pl.BlockSpec((3, 3 * DN_W), const),
        pl.BlockSpec((1, LANES), const),
        pl.BlockSpec((1, LANES), const),
    ]
    out_shape = (
        jax.ShapeDtypeStruct((batch, N_Q_HEADS, HEAD_DIM, seq_len), BF16),
        jax.ShapeDtypeStruct((batch, seq_len, ATT_KV), BF16),
        jax.ShapeDtypeStruct((batch, seq_len // ATT_VT, ATT_KV, ATT_VT), BF16),
        jax.ShapeDtypeStruct((n, DN_W), BF16),
        jax.ShapeDtypeStruct((n, DN_W), BF16),
        jax.ShapeDtypeStruct((n, DN_W), BF16),
        jax.ShapeDtypeStruct((n, DN_W), BF16),
        jax.ShapeDtypeStruct((n, 2 * D_MODEL), BF16),
        jax.ShapeDtypeStruct((n, LANES), F32),
        jax.ShapeDtypeStruct((batch, 4 * N_DN_HEADS, seq_len), F32),
    )
    kvc = tm // ATT_VT
    out_specs = (
        pl.BlockSpec((1, N_Q_HEADS, HEAD_DIM, tm), lambda i: (i // tps, 0, 0, i % tps)),
        pl.BlockSpec((1, tm, ATT_KV), lambda i: (i // tps, i % tps, 0)),
        pl.BlockSpec((1, kvc, ATT_KV, ATT_VT), lambda i: (i // tps, i % tps, 0, 0)),
        pl.BlockSpec((tm, DN_W), tok),
        pl.BlockSpec((tm, DN_W), tok),
        pl.BlockSpec((tm, DN_W), tok),
        pl.BlockSpec((tm, DN_W), tok),
        pl.BlockSpec((tm, 2 * D_MODEL), tok),
        pl.BlockSpec((tm, LANES), tok),
        pl.BlockSpec((1, 4 * N_DN_HEADS, tm), lambda i: (i // tps, 0, i % tps)),
    )
    return pl.pallas_call(
        functools.partial(_proj_kernel, tiles_per_seq=tps),
        grid=(n // tm,),
        in_specs=in_specs,
        out_specs=out_specs,
        out_shape=out_shape,
        compiler_params=pltpu.CompilerParams(dimension_semantics=("arbitrary",),
                                             vmem_limit_bytes=VMEM_LIMIT_BYTES),
        name="proj",
    )(x2d, x2d, x2d, g1, w_pack, rope_t, qg_t, kg_t, conv_w, alog_row, dtb_row)


def _attn_kernel(q_ref, k_ref, v_ref, o_ref, s_scr, acc_scr, *, n_kv_tiles):
    kvh = pl.program_id(1)
    tq = q_ref.shape[3]
    nq = GQA_GROUP * tq
    q4 = q_ref[0]
    qcat = jnp.concatenate([q4[g] for g in range(GQA_GROUP)], axis=1).astype(F32)
    zero = jnp.zeros_like(qcat)
    qpad = jnp.where(kvh == 0, jnp.concatenate([qcat, zero], axis=0),
                     jnp.concatenate([zero, qcat], axis=0)).astype(BF16)
    kvt = s_scr.shape[1]
    vt_per_tile = kvt // ATT_VT

    def scores(slot, j):
        kt = k_ref[0, pl.ds(pl.multiple_of(j * kvt, kvt), kvt), :]
        s = _dot(kt, qpad)
        s_scr[slot] = s
        return jnp.max(s, axis=0, keepdims=True)

    def update(slot, j, m, l, tile_max):
        m_new = jnp.maximum(m, tile_max)
        alpha = jnp.exp2(m - m_new)
        p = jnp.exp2(s_scr[slot] - m_new)
        l = alpha * l + jnp.sum(p, axis=0, keepdims=True)
        vt = jnp.concatenate([v_ref[0, j * vt_per_tile + c] for c in range(vt_per_tile)], axis=1)
        acc_scr[...] = alpha * acc_scr[...] + _dot(vt, p.astype(BF16))
        return m_new, l

    acc_scr[...] = jnp.zeros_like(acc_scr)
    ahead = ATT_SLOTS // 2
    maxes = tuple(scores(t, t) for t in range(ahead))

    def group(j0, m, l, maxes, produce_next):
        maxes = list(maxes) + [None] * (ATT_SLOTS - ahead)
        for t in range(ATT_SLOTS):
            tp = t + ahead
            if tp < ATT_SLOTS or produce_next:
                maxes[tp % ATT_SLOTS] = scores(tp % ATT_SLOTS, j0 + tp)
            m, l = update(t, j0 + t, m, l, maxes[t])
        return m, l, tuple(maxes[:ahead])

    init = (jnp.full((1, nq), NEG_BIG, F32), jnp.zeros((1, nq), F32), maxes)
    n_groups = n_kv_tiles // ATT_SLOTS
    m, l, maxes = lax.fori_loop(0, n_groups - 1, lambda jj, c: group(ATT_SLOTS * jj, *c, True), init)
    _, l, _ = group(ATT_SLOTS * (n_groups - 1), m, l, maxes, False)
    out = acc_scr[...] * (1.0 / l)
    out = jnp.concatenate([out[:, g * tq:(g + 1) * tq] for g in range(GQA_GROUP)], axis=0)
    o_ref[0] = out.T.astype(BF16)


def _attn_call(qt, k, vt):
    batch, _, _, seq_len = qt.shape
    kvt = min(ATT_KVT_MAX, seq_len // ATT_MIN_TILES)
    tq = ATT_TILE_SCORES // (kvt * GQA_GROUP)
    nkv = seq_len // kvt
    assert nkv % ATT_SLOTS == 0 and kvt % ATT_VT == 0
    nq = GQA_GROUP * tq
    return pl.pallas_call(
        functools.partial(_attn_kernel, n_kv_tiles=nkv),
        grid=(batch, N_KV_HEADS, seq_len // tq),
        in_specs=[
            pl.BlockSpec((1, GQA_GROUP, HEAD_DIM, tq), lambda b, h, i: (b, h, 0, i)),
            pl.BlockSpec((1, seq_len, ATT_KV), lambda b, h, i: (b, 0, 0)),
            pl.BlockSpec((1, seq_len // ATT_VT, HEAD_DIM, ATT_VT), lambda b, h, i: (b, 0, h, 0)),
        ],
        out_specs=pl.BlockSpec((1, tq, GQA_GROUP * HEAD_DIM), lambda b, h, i: (b, i, h)),
        out_shape=jax.ShapeDtypeStruct((batch, seq_len, ATT_Q), BF16),
        scratch_shapes=[pltpu.VMEM((ATT_SLOTS, kvt, nq), F32), pltpu.VMEM((HEAD_DIM, nq), F32)],
        compiler_params=pltpu.CompilerParams(
            dimension_semantics=("arbitrary", "arbitrary", "arbitrary"),
            vmem_limit_bytes=VMEM_LIMIT_BYTES),
        name="attn",
    )(qt, k, vt)


def _split3(x):
    p1 = x.astype(BF16)
    r1 = x - p1.astype(F32)
    p2 = r1.astype(BF16)
    p3 = (r1 - p2.astype(F32)).astype(BF16)
    return p1, p2, p3


def _delta_kernel(qf_ref, kf_ref, vf_ref, bgf_ref, bgtf_ref, qb_ref, kb_ref, vb_ref, bgb_ref, bgtb_ref,
                  of_ref, ob_ref, s_ref):
    @pl.when(pl.program_id(1) == 0)
    def _():
        s_ref[...] = jnp.zeros_like(s_ref)

    blk = DN_BLK
    nch = blk // DN_CHUNK
    ri = lax.broadcasted_iota(jnp.int32, (blk, blk), 0)
    ci = lax.broadcasted_iota(jnp.int32, (blk, blk), 1)
    same = (ri // DN_CHUNK) == (ci // DN_CHUNK)
    lower_incl, lower_strict = same & (ri >= ci), same & (ri > ci)
    upper_incl, upper_strict = same & (ri <= ci), same & (ri < ci)
    eye = (ri == ci).astype(F32)
    dir_refs = ((qf_ref, kf_ref, vf_ref, bgf_ref, bgtf_ref, of_ref),
                (qb_ref, kb_ref, vb_ref, bgb_ref, bgtb_ref, ob_ref))
    n_streams = DN_SEQS * DN_DIRS
    refs = [tuple(r.at[0, d // DN_DIRS] for r in dir_refs[d % DN_DIRS]) for d in range(n_streams)]

    dirs = []
    for d in range(n_streams):
        reverse = d % DN_DIRS == 1
        incl, strict, incl_t = ((upper_incl, upper_strict, lower_incl) if reverse
                                else (lower_incl, lower_strict, upper_incl))
        bg = refs[d][3][...]
        bgt = refs[d][4][...]
        m_col = jnp.where(incl, 1.0, 0.0).astype(BF16)
        m_row = jnp.where(incl_t, 1.0, 0.0).astype(BF16)
        gc = sum(_dot(m_col, piece) for piece in _split3(bg))
        gct = sum(_dot(piece, m_row) for piece in _split3(bgt))
        last_row = [(c * DN_CHUNK if reverse else (c + 1) * DN_CHUNK - 1) for c in range(nch)]
        order = list(range(nch - 1, -1, -1)) if reverse else list(range(nch))
        dirs.append(dict(incl=incl, strict=strict, bg=bg, gc=gc, gct=gct, last_row=last_row, order=order))

    chains = [(d, h) for d in range(n_streams) for h in range(N_DN_HEADS)]
    nc = len(chains)

    qh, kh, vh, kk, qk = [], [], [], [], []
    for d, h in chains:
        lo = h * DN_HEAD_DIM
        qh.append(refs[d][0][:, lo:lo + DN_HEAD_DIM])
        kh.append(refs[d][1][:, lo:lo + DN_HEAD_DIM])
        vh.append(refs[d][2][:, lo:lo + DN_HEAD_DIM])
    for i in range(nc):
        kk.append(_dot_nt(kh[i], kh[i]))
        qk.append(_dot_nt(qh[i], kh[i]))

    low, qkd, rhs, q_dec, k_dec, g_cols = [], [], [], [], [], []
    for i, (d, h) in enumerate(chains):
        dd = dirs[d]
        cb = (d % DN_DIRS) * N_DN_HEADS + h
        cg = 2 * N_DN_HEADS + cb
        b_col = dd["bg"][:, cb:cb + 1]
        g_col = dd["gc"][:, cg:cg + 1]
        g_row = dd["gct"][cg:cg + 1, :]
        decay = jnp.exp(jnp.where(dd["incl"], g_col - g_row, NEG_BIG))
        low.append(jnp.where(dd["strict"], b_col * kk[i] * decay, 0.0))
        qkd.append((qk[i] * decay).astype(BF16))
        e_col = jnp.exp(g_col)
        kf = kh[i].astype(F32)
        rhs.append(jnp.concatenate([vh[i].astype(F32) * b_col, kf * (b_col * e_col)], axis=1).astype(BF16))
        q_dec.append((qh[i].astype(F32) * e_col).astype(BF16))
        gl_col = jnp.concatenate(
            [jnp.broadcast_to(g_col[r:r + 1], (DN_CHUNK, 1)) for r in dd["last_row"]], axis=0)
        k_dec.append(kf * jnp.exp(gl_col - g_col))
        g_cols.append(g_col)

    def compact(x):
        return sum(x[c * DN_CHUNK:(c + 1) * DN_CHUNK] for c in range(nch))

    def expand(xc):
        return jnp.where(same, jnp.concatenate([xc] * nch, axis=0), 0.0).astype(BF16)

    eye_c = compact(eye)
    low_c = [compact(x) for x in low]
    xc = [_dot(low_c[i].astype(BF16), low[i].astype(BF16)) for i in range(nc)]
    rc = [eye_c - low_c[i] for i in range(nc)]
    n_sq = int(math.log2(DN_CHUNK)) - 1
    for it in range(n_sq):
        xbd = [expand(x) for x in xc]
        if it < n_sq - 1:
            both = [_dot(jnp.concatenate([rc[i], xc[i]], axis=0).astype(BF16), xbd[i]) for i in range(nc)]
            rc = [rc[i] + both[i][:DN_CHUNK] for i in range(nc)]
            xc = [both[i][DN_CHUNK:] for i in range(nc)]
        else:
            rc = [rc[i] + _dot(rc[i].astype(BF16), xbd[i]) for i in range(nc)]

    uw = [_dot(expand(rc[i]), rhs[i]) for i in range(nc)]
    u = [x[:, :DN_HEAD_DIM] for x in uw]
    w = [x[:, DN_HEAD_DIM:].astype(BF16) for x in uw]

    npair = nc // 2
    dk2 = 2 * DN_HEAD_DIM
    left = lax.broadcasted_iota(jnp.int32, (DN_HEAD_DIM, dk2), 1) < DN_HEAD_DIM
    state = [s_ref[pi] for pi in range(npair)]
    v_new = [[None] * nch for _ in range(nc)]
    o_q = [[None] * nch for _ in range(nc)]
    zeros_v = jnp.zeros((DN_CHUNK, DN_HEAD_DIM), BF16)
    chunk_of = lambda pi, step: dirs[chains[2 * pi][0]]["order"][step]
    kd_pair_t = [[jnp.concatenate([k_dec[2 * pi][c * DN_CHUNK:(c + 1) * DN_CHUNK],
                                   k_dec[2 * pi + 1][c * DN_CHUNK:(c + 1) * DN_CHUNK]], axis=0).T.astype(BF16)
                  for c in range(nch)] for pi in range(npair)]
    for step in range(nch):
        res = []
        for pi in range(npair):
            a, b = 2 * pi, 2 * pi + 1
            r0 = chunk_of(pi, step) * DN_CHUNK
            s_bd = jnp.concatenate([jnp.where(left, state[pi], 0.0), jnp.where(left, 0.0, state[pi])],
                                   axis=0).astype(BF16)
            lhs = jnp.concatenate(
                [jnp.concatenate([w[i][r0:r0 + DN_CHUNK], q_dec[i][r0:r0 + DN_CHUNK]], axis=0) for i in (a, b)],
                axis=1)
            res.append(_dot(lhs, s_bd))
        for pi in range(npair):
            c = chunk_of(pi, step)
            r0 = c * DN_CHUNK
            for side, i in enumerate((2 * pi, 2 * pi + 1)):
                part = res[pi][:, side * DN_HEAD_DIM:(side + 1) * DN_HEAD_DIM]
                v_new[i][c] = u[i][r0:r0 + DN_CHUNK] - part[:DN_CHUNK]
                o_q[i][c] = part[DN_CHUNK:]
        for pi in range(npair):
            a, b = 2 * pi, 2 * pi + 1
            c = chunk_of(pi, step)
            r = dirs[chains[a][0]]["last_row"][c]
            vn_bd = jnp.concatenate(
                [jnp.concatenate([v_new[a][c].astype(BF16), zeros_v], axis=1),
                 jnp.concatenate([zeros_v, v_new[b][c].astype(BF16)], axis=1)], axis=0)
            keep = jnp.where(left, jnp.exp(g_cols[a][r:r + 1]), jnp.exp(g_cols[b][r:r + 1]))
            state[pi] = state[pi] * keep + _dot(kd_pair_t[pi][c], vn_bd)
    for pi in range(npair):
        s_ref[pi] = state[pi]

    for i, (d, h) in enumerate(chains):
        lo = h * DN_HEAD_DIM
        vn_all = jnp.concatenate(v_new[i], axis=0).astype(BF16)
        o = jnp.concatenate(o_q[i], axis=0) + _dot(qkd[i], vn_all)
        refs[d][5][:, lo:lo + DN_HEAD_DIM] = o.astype(BF16)


def _delta_call(dq, dk, dv, bg, bgt, seq_len):
    n = dq.shape[0]
    batch = n // seq_len
    nb = seq_len // DN_BLK
    groups = batch // DN_SEQS
    dq, dk, dv = (a.reshape(groups, DN_SEQS, seq_len, DN_W) for a in (dq, dk, dv))
    bg = bg.reshape(groups, DN_SEQS, seq_len, LANES)
    bgt = bgt.reshape(groups, DN_SEQS, 4 * N_DN_HEADS, seq_len)
    fwd = lambda b, i: (b, 0, i, 0)
    bwd = lambda b, i: (b, 0, nb - 1 - i, 0)
    fwd_t = lambda b, i: (b, 0, 0, i)
    bwd_t = lambda b, i: (b, 0, 0, nb - 1 - i)
    tok = lambda m: pl.BlockSpec((1, DN_SEQS, DN_BLK, DN_W), m)
    in_specs = [tok(fwd), tok(fwd), tok(fwd), pl.BlockSpec((1, DN_SEQS, DN_BLK, LANES), fwd),
                pl.BlockSpec((1, DN_SEQS, 4 * N_DN_HEADS, DN_BLK), fwd_t),
                tok(bwd), tok(bwd), tok(bwd), pl.BlockSpec((1, DN_SEQS, DN_BLK, LANES), bwd),
                pl.BlockSpec((1, DN_SEQS, 4 * N_DN_HEADS, DN_BLK), bwd_t)]
    o_shape = jax.ShapeDtypeStruct((groups, DN_SEQS, seq_len, DN_W), BF16)
    o_f, o_b = pl.pallas_call(
        _delta_kernel,
        grid=(groups, nb),
        in_specs=in_specs,
        out_specs=(tok(fwd), tok(bwd)),
        out_shape=(o_shape, o_shape),
        scratch_shapes=[pltpu.VMEM((DN_SEQS * DN_DIRS * N_DN_HEADS // 2, DN_HEAD_DIM, 2 * DN_HEAD_DIM), F32)],
        compiler_params=pltpu.CompilerParams(dimension_semantics=("arbitrary", "arbitrary"),
                                             vmem_limit_bytes=VMEM_LIMIT_BYTES),
        name="delta",
    )(dq, dk, dv, bg, bgt, dq, dk, dv, bg, bgt)
    return o_f.reshape(n, DN_W), o_b.reshape(n, DN_W)


def _out_kernel(x_ref, attn_ref, of_ref, ob_ref, sz_ref, gate_ref, dng_ref, wa_ref, wb_ref, wo_ref,
                g1p_ref, g2_ref, w1_ref, w2_ref, g2p_ref, y_ref):
    o = of_ref[...].astype(F32) + ob_ref[...].astype(F32)
    dng = dng_ref[...]
    dn = jnp.concatenate([_rms(o[:, h * DN_HEAD_DIM:(h + 1) * DN_HEAD_DIM], dng)
                          for h in range(N_DN_HEADS)], axis=1)
    dn = (dn * sz_ref[...].astype(F32)).astype(BF16)
    a = _dot(attn_ref[...], wa_ref[...])
    dd = _dot(dn, wb_ref[...])
    gate = gate_ref[...].astype(F32)
    merged = (gate[:, :D_MODEL] * a + gate[:, D_MODEL:] * dd).astype(BF16)
    h1 = x_ref[...] + _rms(_dot(merged, wo_ref[...]), g1p_ref[...])
    hid = _dot(_rms(h1, g2_ref[...]).astype(BF16), w1_ref[...])
    hid = jnp.square(jnp.maximum(hid, 0.0)).astype(BF16)
    y_ref[...] = h1 + _rms(_dot(hid, w2_ref[...]), g2p_ref[...])


def _out_call(x2d, attn2d, o_f, o_b, sz, gate, dng, wa, wb, wo, g1p, g2, w1, w2, g2p):
    n = x2d.shape[0]
    tm = OUT_TM
    tok = lambda w: pl.BlockSpec((tm, w), lambda i: (i, 0))
    full = lambda a: pl.BlockSpec(a.shape, lambda i: (0, 0), pipeline_mode=pl.Buffered(1))
    return pl.pallas_call(
        _out_kernel,
        grid=(n // tm,),
        in_specs=[tok(D_MODEL), tok(ATT_Q), tok(DN_W), tok(DN_W), tok(DN_W), tok(2 * D_MODEL),
                  full(dng), full(wa), full(wb), full(wo), full(g1p), full(g2), full(w1), full(w2), full(g2p)],
        out_specs=tok(D_MODEL),
        out_shape=jax.ShapeDtypeStruct((n, D_MODEL), F32),
        compiler_params=pltpu.CompilerParams(dimension_semantics=("arbitrary",),
                                             vmem_limit_bytes=VMEM_LIMIT_BYTES),
        name="outmlp",
    )(x2d, attn2d, o_f, o_b, sz, gate, dng, wa, wb, wo, g1p, g2, w1, w2, g2p)


def _rope_table_t(seq_len):
    pos = jnp.arange(seq_len, dtype=jnp.int32)
    row_ids = (pos // GRID_W).astype(F32)
    col_ids = (pos % GRID_W).astype(F32)
    inv_freq = ROPE_THETA ** (-jnp.arange(0, ROPE_HALF, 2, dtype=F32) / ROPE_HALF)
    ang_r = inv_freq[:, None] * row_ids[None, :]
    ang_c = inv_freq[:, None] * col_ids[None, :]
    return jnp.concatenate([jnp.cos(ang_r), jnp.sin(ang_r), jnp.cos(ang_c), jnp.sin(ang_c)], axis=0)


def _pack_w_in(w_in):
    aq, ak, av, dqkv, dz, dbeta, da, gates = jnp.split(
        w_in, [ATT_Q, ATT_Q + ATT_KV, ATT_Q + 2 * ATT_KV, C_Z, C_Z + DN_W,
               C_Z + DN_W + 2 * N_DN_HEADS, C_Z + DN_W + 4 * N_DN_HEADS], axis=-1)
    pad = jnp.zeros((D_MODEL, LANES - 4 * N_DN_HEADS), w_in.dtype)
    return jnp.concatenate([aq, ak, av, dqkv, dz, gates, dbeta, da, pad], axis=-1).astype(BF16)


def _lane_row(v):
    v = v.reshape(-1).astype(F32)
    return jnp.zeros((1, LANES), F32).at[0, 2 * N_DN_HEADS:4 * N_DN_HEADS].set(v)


def _layer(x, p):
    batch, seq_len, _ = x.shape
    n = batch * seq_len
    x2d = x.reshape(n, D_MODEL)
    rope_t = _rope_table_t(seq_len)
    qt, k, vt, dq, dk, dv, sz, gate, bg, bgt = _proj_call(
        x2d, seq_len, p["g1"], p["w_pack"], rope_t, p["qg_t"], p["kg_t"], p["conv_w"], p["alog"], p["dtb"])
    attn = _attn_call(qt, k, vt).reshape(n, ATT_Q)
    o_f, o_b = _delta_call(dq, dk, dv, bg, bgt, seq_len)
    y = _out_call(x2d, attn, o_f, o_b, sz, gate, p["dng"], p["wa"], p["wb"], p["wo"],
                  p["g1p"], p["g2"], p["w1"], p["w2"], p["g2p"])
    return y.reshape(batch, seq_len, D_MODEL)


def kernel(x_prompt, x_sample, ln1_pre_g, w_in, attn_q_norm_g, attn_k_norm_g, dn_conv_w, dn_A_log, dn_dt_bias, dn_out_norm_g, w_attn_branch, w_dn_branch, w_out, ln1_post_g, ln2_pre_g, w_ff_in, w_ff_out, ln2_post_g):
    depth = w_in.shape[0]
    outs = []
    for x in (x_prompt, x_sample):
        for l in range(depth):
            p = {
                "g1": ln1_pre_g[l].reshape(1, D_MODEL),
                "w_pack": _pack_w_in(w_in[l]),
                "qg_t": jnp.broadcast_to(attn_q_norm_g[l][:, None], (HEAD_DIM, PROJ_TM)),
                "kg_t": jnp.broadcast_to(attn_k_norm_g[l][:, None], (HEAD_DIM, PROJ_TM)),
                "conv_w": dn_conv_w[l],
                "alog": _lane_row(dn_A_log[l]),
                "dtb": _lane_row(dn_dt_bias[l]),
                "dng": dn_out_norm_g[l].reshape(1, DN_HEAD_DIM),
                "wa": w_attn_branch[l].astype(BF16),
                "wb": w_dn_branch[l].astype(BF16),
                "wo": w_out[l].astype(BF16),
                "g1p": ln1_post_g[l].reshape(1, D_MODEL),
                "g2": ln2_pre_g[l].reshape(1, D_MODEL),
                "w1": w_ff_in[l].astype(BF16),
                "w2": w_ff_out[l].astype(BF16),
                "g2p": ln2_post_g[l].reshape(1, D_MODEL),
            }
            x = _layer(x, p)
        outs.append(x)
    return tuple(outs)
```

```python
import functools
import math

import jax
import jax.numpy as jnp
from jax import lax
from jax.experimental import pallas as pl
from jax.experimental.pallas import tpu as pltpu

D_MODEL = 1024
GRID_W = 64
N_Q_HEADS = 8
N_KV_HEADS = 2
HEAD_DIM = 64
GQA_GROUP = N_Q_HEADS // N_KV_HEADS
ROPE_HALF = HEAD_DIM // 2
ROPE_FREQS = ROPE_HALF // 2
ROPE_THETA = 10000.0
N_DN_HEADS = 4
DN_HEAD_DIM = 128
DN_CHUNK = 64
D_FF = 4 * D_MODEL
EPS = 1e-6
ATT_Q = N_Q_HEADS * HEAD_DIM
ATT_KV = N_KV_HEADS * HEAD_DIM
DN_W = N_DN_HEADS * DN_HEAD_DIM

LANES = 128
SUBLANES = 8
VMEM_LIMIT_BYTES = 56 * 1024 * 1024

C_Q = 0
C_K = C_Q + ATT_Q
C_V = C_K + ATT_KV
C_D = C_V + ATT_KV
C_Z = C_D + 3 * DN_W
C_G = C_Z + DN_W
C_B = C_G + 2 * D_MODEL
PACK_W = C_B + LANES

PROJ_TM = 512
OUT_TM = 512
ATT_TILE_SCORES = 1024 * 2048
ATT_KVT_MAX = 1024
ATT_MIN_TILES = 8
ATT_VT = 256
ATT_SLOTS = 2
DN_BLK = 4 * DN_CHUNK
DN_DIRS = 2
DN_SEQS = 2

F32 = jnp.float32
BF16 = jnp.bfloat16
NEG_BIG = -1e30


def _rms(x, g):
    ms = jnp.mean(x * x, axis=-1, keepdims=True)
    return x * lax.rsqrt(ms + EPS) * g


def _sigmoid(x):
    return 1.0 / (1.0 + jnp.exp2(x * -math.log2(math.e)))


def _dot(a, b):
    return jnp.dot(a, b, preferred_element_type=F32)


def _dot_nt(a, b):
    return lax.dot_general(a, b, (((1,), (1,)), ((), ())), preferred_element_type=F32)


def _norm_rope_t(xt, gain_t, rope_t):
    ms = jnp.mean(xt * xt, axis=0, keepdims=True)
    xt = xt * lax.rsqrt(ms + EPS) * gain_t
    f = ROPE_FREQS
    x1r, x2r, x1c, x2c = xt[0:f], xt[f:2 * f], xt[2 * f:3 * f], xt[3 * f:4 * f]
    cr, sr, cc, sc = rope_t[0:f], rope_t[f:2 * f], rope_t[2 * f:3 * f], rope_t[3 * f:4 * f]
    return jnp.concatenate([x1r * cr - x2r * sr, x2r * cr + x1r * sr,
                            x1c * cc - x2c * sc, x2c * cc + x1c * sc], axis=0)


def _proj_kernel(x_ref, xp_ref, xn_ref, g1_ref, w_ref, rope_ref, qg_ref, kg_ref, conv_ref,
                 alog_ref, dtb_ref,
                 qt_ref, k_ref, vt_ref, dq_ref, dk_ref, dv_ref, sz_ref, gate_ref, bg_ref, bgt_ref,
                 *, tiles_per_seq):
    tm = x_ref.shape[0]
    ti = pl.program_id(0) % tiles_per_seq
    g1 = g1_ref[...]
    xb = _rms(x_ref[...], g1).astype(BF16)

    halo = _rms(jnp.concatenate([xn_ref[...], xp_ref[...]], axis=0), g1).astype(BF16)
    y_ext = _dot(jnp.concatenate([xb, halo], axis=0), w_ref[:, C_D:C_D + 3 * DN_W])
    y = y_ext[0:tm]
    y_nxt = jnp.where(ti == tiles_per_seq - 1, 0.0, y_ext[tm:tm + SUBLANES])
    y_prv = jnp.where(ti == 0, 0.0, y_ext[tm + SUBLANES:])
    y_ext = jnp.concatenate([y, y_nxt, y_prv], axis=0)
    rows = tm + 2 * SUBLANES
    cw = conv_ref[...]
    c = (cw[0:1] * pltpu.roll(y_ext, 1, axis=0)[0:tm] + cw[1:2] * y
         + cw[2:3] * pltpu.roll(y_ext, rows - 1, axis=0)[0:tm])
    s = c * _sigmoid(c)
    for part, ref, scale in ((0, dq_ref, DN_HEAD_DIM ** -0.5), (1, dk_ref, 1.0)):
        outs = []
        for h in range(N_DN_HEADS):
            lo = part * DN_W + h * DN_HEAD_DIM
            xh = s[:, lo:lo + DN_HEAD_DIM]
            ss = jnp.sum(xh * xh, axis=-1, keepdims=True)
            outs.append(xh * (lax.rsqrt(ss + EPS) * scale))
        ref[...] = jnp.concatenate(outs, axis=1).astype(BF16)
    dv_ref[...] = s[:, 2 * DN_W:3 * DN_W].astype(BF16)

    z = _dot(xb, w_ref[:, C_Z:C_Z + DN_W])
    sz_ref[...] = (z * _sigmoid(z)).astype(BF16)
    gate_ref[...] = _sigmoid(_dot(xb, w_ref[:, C_G:C_G + 2 * D_MODEL])).astype(BF16)

    rope_t = rope_ref[...]
    yq_t = _dot(xb, w_ref[:, C_Q:C_Q + ATT_Q]).T
    q_scale = (HEAD_DIM ** -0.5) * math.log2(math.e)
    qg = qg_ref[...]
    for h in range(N_Q_HEADS):
        qh = _norm_rope_t(yq_t[h * HEAD_DIM:(h + 1) * HEAD_DIM], qg, rope_t)
        qt_ref[0, h] = (qh * q_scale).astype(BF16)

    yk_t = _dot(xb, w_ref[:, C_K:C_K + ATT_KV]).T
    kg = kg_ref[...]
    k_t = jnp.concatenate([_norm_rope_t(yk_t[h * HEAD_DIM:(h + 1) * HEAD_DIM], kg, rope_t)
                           for h in range(N_KV_HEADS)], axis=0)
    k_ref[0] = k_t.T.astype(BF16)
    yv_t = _dot(xb, w_ref[:, C_V:C_V + ATT_KV]).T.astype(BF16)
    for c in range(tm // ATT_VT):
        vt_ref[0, c] = yv_t[:, c * ATT_VT:(c + 1) * ATT_VT]

    yb = _dot(xb, w_ref[:, C_B:C_B + LANES])
    lane = lax.broadcasted_iota(jnp.int32, (1, LANES), 1)
    t = yb + dtb_ref[...]
    softplus = jnp.maximum(t, 0.0) + jnp.log1p(jnp.exp(-jnp.abs(t)))
    bg = jnp.where(lane < 2 * N_DN_HEADS, _sigmoid(yb), -jnp.exp(alog_ref[...]) * softplus)
    bg = jnp.where(lane < 4 * N_DN_HEADS, bg, 0.0)
    bg_ref[...] = bg
    bgt_ref[0] = bg.T[0:4 * N_DN_HEADS]


def _proj_call(x2d, seq_len, g1, w_pack, rope_t, qg_t, kg_t, conv_w, alog_row, dtb_row):
    n = x2d.shape[0]
    tm = PROJ_TM
    batch = n // seq_len
    tps = seq_len // tm
    hb = tm // SUBLANES
    n8 = n // SUBLANES
    const = lambda i: (0, 0)
    tok = lambda i: (i, 0)
    in_specs = [
        pl.BlockSpec((tm, D_MODEL), tok),
        pl.BlockSpec((SUBLANES, D_MODEL), lambda i: (jnp.maximum(i * hb - 1, 0), 0)),
        pl.BlockSpec((SUBLANES, D_MODEL), lambda i: (jnp.minimum((i + 1) * hb, n8 - 1), 0)),
        pl.BlockSpec((1, D_MODEL), const),
        pl.BlockSpec((D_MODEL, PACK_W), const, pipeline_mode=pl.Buffered(1)),
        pl.BlockSpec((HEAD_DIM, tm), lambda i: (0, i % tps)),
        pl.BlockSpec((HEAD_DIM, tm), const),
        pl.BlockSpec((HEAD_DIM, tm), const),
        pl.BlockSpec((3, 3 * DN_W), const),
        pl.BlockSpec((1, LANES), const),
        pl.BlockSpec((1, LANES), const),
    ]
    out_shape = (
        jax.ShapeDtypeStruct((batch, N_Q_HEADS, HEAD_DIM, seq_len), BF16),
        jax.ShapeDtypeStruct((batch, seq_len, ATT_KV), BF16),
        jax.ShapeDtypeStruct((batch, seq_len // ATT_VT, ATT_KV, ATT_VT), BF16),
        jax.ShapeDtypeStruct((n, DN_W), BF16),
        jax.ShapeDtypeStruct((n, DN_W), BF16),
        jax.ShapeDtypeStruct((n, DN_W), BF16),
        jax.ShapeDtypeStruct((n, DN_W), BF16),
        jax.ShapeDtypeStruct((n, 2 * D_MODEL), BF16),
        jax.ShapeDtypeStruct((n, LANES), F32),
        jax.ShapeDtypeStruct((batch, 4 * N_DN_HEADS, seq_len), F32),
    )
    kvc = tm // ATT_VT
    out_specs = (
        pl.BlockSpec((1, N_Q_HEADS, HEAD_DIM, tm), lambda i: (i // tps, 0, 0, i % tps)),
        pl.BlockSpec((1, tm, ATT_KV), lambda i: (i // tps, i % tps, 0)),
        pl.BlockSpec((1, kvc, ATT_KV, ATT_VT), lambda i: (i // tps, i % tps, 0, 0)),
        pl.BlockSpec((tm, DN_W), tok),
        pl.BlockSpec((tm, DN_W), tok),
        pl.BlockSpec((tm, DN_W), tok),
        pl.BlockSpec((tm, DN_W), tok),
        pl.BlockSpec((tm, 2 * D_MODEL), tok),
        pl.BlockSpec((tm, LANES), tok),
        pl.BlockSpec((1, 4 * N_DN_HEADS, tm), lambda i: (i // tps, 0, i % tps)),
    )
    return pl.pallas_call(
        functools.partial(_proj_kernel, tiles_per_seq=tps),
        grid=(n // tm,),
        in_specs=in_specs,
        out_specs=out_specs,
        out_shape=out_shape,
        compiler_params=pltpu.CompilerParams(dimension_semantics=("arbitrary",),
                                             vmem_limit_bytes=VMEM_LIMIT_BYTES),
        name="proj",
    )(x2d, x2d, x2d, g1, w_pack, rope_t, qg_t, kg_t, conv_w, alog_row, dtb_row)


def _attn_kernel(q_ref, k_ref, v_ref, o_ref, s_scr, acc_scr, *, n_kv_tiles):
    kvh = pl.program_id(1)
    tq = q_ref.shape[3]
    nq = GQA_GROUP * tq
    q4 = q_ref[0]
    qcat = jnp.concatenate([q4[g] for g in range(GQA_GROUP)], axis=1).astype(F32)
    zero = jnp.zeros_like(qcat)
    qpad = jnp.where(kvh == 0, jnp.concatenate([qcat, zero], axis=0),
                     jnp.concatenate([zero, qcat], axis=0)).astype(BF16)
    kvt = s_scr.shape[1]
    vt_per_tile = kvt // ATT_VT

    def scores(slot, j):
        kt = k_ref[0, pl.ds(pl.multiple_of(j * kvt, kvt), kvt), :]
        s = _dot(kt, qpad)
        s_scr[slot] = s
        return jnp.max(s, axis=0, keepdims=True)

    def update(slot, j, m, l, tile_max):
        m_new = jnp.maximum(m, tile_max)
        alpha = jnp.exp2(m - m_new)
        p = jnp.exp2(s_scr[slot] - m_new)
        l = alpha * l + jnp.sum(p, axis=0, keepdims=True)
        vt = jnp.concatenate([v_ref[0, j * vt_per_tile + c] for c in range(vt_per_tile)], axis=1)
        acc_scr[...] = alpha * acc_scr[...] + _dot(vt, p.astype(BF16))
        return m_new, l

    acc_scr[...] = jnp.zeros_like(acc_scr)
    ahead = ATT_SLOTS // 2
    maxes = tuple(scores(t, t) for t in range(ahead))

    def group(j0, m, l, maxes, produce_next):
        maxes = list(maxes) + [None] * (ATT_SLOTS - ahead)
        for t in range(ATT_SLOTS):
            tp = t + ahead
            if tp < ATT_SLOTS or produce_next:
                maxes[tp % ATT_SLOTS] = scores(tp % ATT_SLOTS, j0 + tp)
            m, l = update(t, j0 + t, m, l, maxes[t])
        return m, l, tuple(maxes[:ahead])

    init = (jnp.full((1, nq), NEG_BIG, F32), jnp.zeros((1, nq), F32), maxes)
    n_groups = n_kv_tiles // ATT_SLOTS
    m, l, maxes = lax.fori_loop(0, n_groups - 1, lambda jj, c: group(ATT_SLOTS * jj, *c, True), init)
    _, l, _ = group(ATT_SLOTS * (n_groups - 1), m, l, maxes, False)
    out = acc_scr[...] * (1.0 / l)
    out = jnp.concatenate([out[:, g * tq:(g + 1) * tq] for g in range(GQA_GROUP)], axis=0)
    o_ref[0] = out.T.astype(BF16)


def _attn_call(qt, k, vt):
    batch, _, _, seq_len = qt.shape
    kvt = min(ATT_KVT_MAX, seq_len // ATT_MIN_TILES)
    tq = ATT_TILE_SCORES // (kvt * GQA_GROUP)
    nkv = seq_len // kvt
    assert nkv % ATT_SLOTS == 0 and kvt % ATT_VT == 0
    nq = GQA_GROUP * tq
    return pl.pallas_call(
        functools.partial(_attn_kernel, n_kv_tiles=nkv),
        grid=(batch, N_KV_HEADS, seq_len // tq),
        in_specs=[
            pl.BlockSpec((1, GQA_GROUP, HEAD_DIM, tq), lambda b, h, i: (b, h, 0, i)),
            pl.BlockSpec((1, seq_len, ATT_KV), lambda b, h, i: (b, 0, 0)),
            pl.BlockSpec((1, seq_len // ATT_VT, HEAD_DIM, ATT_VT), lambda b, h, i: (b, 0, h, 0)),
        ],
        out_specs=pl.BlockSpec((1, tq, GQA_GROUP * HEAD_DIM), lambda b, h, i: (b, i, h)),
        out_shape=jax.ShapeDtypeStruct((batch, seq_len, ATT_Q), BF16),
        scratch_shapes=[pltpu.VMEM((ATT_SLOTS, kvt, nq), F32), pltpu.VMEM((HEAD_DIM, nq), F32)],
        compiler_params=pltpu.CompilerParams(
            dimension_semantics=("arbitrary", "arbitrary", "arbitrary"),
            vmem_limit_bytes=VMEM_LIMIT_BYTES),
        name="attn",
    )(qt, k, vt)


def _split3(x):
    p1 = x.astype(BF16)
    r1 = x - p1.astype(F32)
    p2 = r1.astype(BF16)
    p3 = (r1 - p2.astype(F32)).astype(BF16)
    return p1, p2, p3


def _delta_kernel(qf_ref, kf_ref, vf_ref, bgf_ref, bgtf_ref, qb_ref, kb_ref, vb_ref, bgb_ref, bgtb_ref,
                  of_ref, ob_ref, s_ref):
    @pl.when(pl.program_id(1) == 0)
    def _():
        s_ref[...] = jnp.zeros_like(s_ref)

    blk = DN_BLK
    nch = blk // DN_CHUNK
    ri = lax.broadcasted_iota(jnp.int32, (blk, blk), 0)
    ci = lax.broadcasted_iota(jnp.int32, (blk, blk), 1)
    same = (ri // DN_CHUNK) == (ci // DN_CHUNK)
    lower_incl, lower_strict = same & (ri >= ci), same & (ri > ci)
    upper_incl, upper_strict = same & (ri <= ci), same & (ri < ci)
    eye = (ri == ci).astype(F32)
    dir_refs = ((qf_ref, kf_ref, vf_ref, bgf_ref, bgtf_ref, of_ref),
                (qb_ref, kb_ref, vb_ref, bgb_ref, bgtb_ref, ob_ref))
    n_streams = DN_SEQS * DN_DIRS
    refs = [tuple(r.at[0, d // DN_DIRS] for r in dir_refs[d % DN_DIRS]) for d in range(n_streams)]

    dirs = []
    for d in range(n_streams):
        reverse = d % DN_DIRS == 1
        incl, strict, incl_t = ((upper_incl, upper_strict, lower_incl) if reverse
                                else (lower_incl, lower_strict, upper_incl))
        bg = refs[d][3][...]
        bgt = refs[d][4][...]
        m_col = jnp.where(incl, 1.0, 0.0).astype(BF16)
        m_row = jnp.where(incl_t, 1.0, 0.0).astype(BF16)
        gc = sum(_dot(m_col, piece) for piece in _split3(bg))
        gct = sum(_dot(piece, m_row) for piece in _split3(bgt))
        last_row = [(c * DN_CHUNK if reverse else (c + 1) * DN_CHUNK - 1) for c in range(nch)]
        order = list(range(nch - 1, -1, -1)) if reverse else list(range(nch))
        dirs.append(dict(incl=incl, strict=strict, bg=bg, gc=gc, gct=gct, last_row=last_row, order=order))

    chains = [(d, h) for d in range(n_streams) for h in range(N_DN_HEADS)]
    nc = len(chains)

    qh, kh, vh, kk, qk = [], [], [], [], []
    for d, h in chains:
        lo = h * DN_HEAD_DIM
        qh.append(refs[d][0][:, lo:lo + DN_HEAD_DIM])
        kh.append(refs[d][1][:, lo:lo + DN_HEAD_DIM])
        vh.append(refs[d][2][:, lo:lo + DN_HEAD_DIM])
    for i in range(nc):
        kk.append(_dot_nt(kh[i], kh[i]))
        qk.append(_dot_nt(qh[i], kh[i]))

    low, qkd, rhs, q_dec, k_dec, g_cols = ([None] * nc for _ in range(6))

    def stage2(i):
        d, h = chains[i]
        dd = dirs[d]
        cb = (d % DN_DIRS) * N_DN_HEADS + h
        cg = 2 * N_DN_HEADS + cb
        b_col = dd["bg"][:, cb:cb + 1]
        g_col = dd["gc"][:, cg:cg + 1]
        g_row = dd["gct"][cg:cg + 1, :]
        decay = jnp.exp(jnp.where(dd["incl"], g_col - g_row, NEG_BIG))
        low[i] = jnp.where(dd["strict"], b_col * kk[i] * decay, 0.0)
        qkd[i] = (qk[i] * decay).astype(BF16)
        e_col = jnp.exp(g_col)
        kf = kh[i].astype(F32)
        rhs[i] = jnp.concatenate([vh[i].astype(F32) * b_col, kf * (b_col * e_col)], axis=1).astype(BF16)
        q_dec[i] = (qh[i].astype(F32) * e_col).astype(BF16)
        gl_col = jnp.concatenate(
            [jnp.broadcast_to(g_col[r:r + 1], (DN_CHUNK, 1)) for r in dd["last_row"]], axis=0)
        k_dec[i] = kf * jnp.exp(gl_col - g_col)
        g_cols[i] = g_col

    def compact(x):
        return sum(x[c * DN_CHUNK:(c + 1) * DN_CHUNK] for c in range(nch))

    def expand(xc):
        return jnp.where(same, jnp.concatenate([xc] * nch, axis=0), 0.0).astype(BF16)

    eye_c = compact(eye)
    n_sq = int(math.log2(DN_CHUNK)) - 1
    u, w = [None] * nc, [None] * nc
    per_seq = nc // DN_SEQS

    def invert(idx):
        for i in idx:
            stage2(i)
        low_c = {i: compact(low[i]) for i in idx}
        xc = {i: _dot(low_c[i].astype(BF16), low[i].astype(BF16)) for i in idx}
        rc = {i: eye_c - low_c[i] for i in idx}
        for it in range(n_sq):
            xbd = {i: expand(xc[i]) for i in idx}
            if it < n_sq - 1:
                both = {i: _dot(jnp.concatenate([rc[i], xc[i]], axis=0).astype(BF16), xbd[i]) for i in idx}
                rc = {i: rc[i] + both[i][:DN_CHUNK] for i in idx}
                xc = {i: both[i][DN_CHUNK:] for i in idx}
            else:
                rc = {i: rc[i] + _dot(rc[i].astype(BF16), xbd[i]) for i in idx}
        for i in idx:
            uw = _dot(expand(rc[i]), rhs[i])
            u[i] = uw[:, :DN_HEAD_DIM]
            w[i] = uw[:, DN_HEAD_DIM:].astype(BF16)

    npair = nc // 2
    dk2 = 2 * DN_HEAD_DIM
    left = lax.broadcasted_iota(jnp.int32, (DN_HEAD_DIM, dk2), 1) < DN_HEAD_DIM
    state = [s_ref[pi] for pi in range(npair)]
    v_new = [[None] * nch for _ in range(nc)]
    o_q = [[None] * nch for _ in range(nc)]
    zeros_v = jnp.zeros((DN_CHUNK, DN_HEAD_DIM), BF16)
    chunk_of = lambda pi, step: dirs[chains[2 * pi][0]]["order"][step]

    def recur(pairs):
        kd_pair_t = {pi: [jnp.concatenate([k_dec[2 * pi][c * DN_CHUNK:(c + 1) * DN_CHUNK],
                                           k_dec[2 * pi + 1][c * DN_CHUNK:(c + 1) * DN_CHUNK]],
                                          axis=0).T.astype(BF16) for c in range(nch)] for pi in pairs}
        for step in range(nch):
            res = {}
            for pi in pairs:
                a, b = 2 * pi, 2 * pi + 1
                r0 = chunk_of(pi, step) * DN_CHUNK
                s_bd = jnp.concatenate([jnp.where(left, state[pi], 0.0), jnp.where(left, 0.0, state[pi])],
                                       axis=0).astype(BF16)
                lhs = jnp.concatenate(
                    [jnp.concatenate([w[i][r0:r0 + DN_CHUNK], q_dec[i][r0:r0 + DN_CHUNK]], axis=0)
                     for i in (a, b)], axis=1)
                res[pi] = _dot(lhs, s_bd)
            for pi in pairs:
                c = chunk_of(pi, step)
                r0 = c * DN_CHUNK
                for side, i in enumerate((2 * pi, 2 * pi + 1)):
                    part = res[pi][:, side * DN_HEAD_DIM:(side + 1) * DN_HEAD_DIM]
                    v_new[i][c] = u[i][r0:r0 + DN_CHUNK] - part[:DN_CHUNK]
                    o_q[i][c] = part[DN_CHUNK:]
            for pi in pairs:
                a, b = 2 * pi, 2 * pi + 1
                c = chunk_of(pi, step)
                r = dirs[chains[a][0]]["last_row"][c]
                vn_bd = jnp.concatenate(
                    [jnp.concatenate([v_new[a][c].astype(BF16), zeros_v], axis=1),
                     jnp.concatenate([zeros_v, v_new[b][c].astype(BF16)], axis=1)], axis=0)
                keep = jnp.where(left, jnp.exp(g_cols[a][r:r + 1]), jnp.exp(g_cols[b][r:r + 1]))
                state[pi] = state[pi] * keep + _dot(kd_pair_t[pi][c], vn_bd)

    for first in range(0, nc, per_seq):
        invert(range(first, first + per_seq))
    recur(range(npair))
    for pi in range(npair):
        s_ref[pi] = state[pi]

    for i, (d, h) in enumerate(chains):
        lo = h * DN_HEAD_DIM
        vn_all = jnp.concatenate(v_new[i], axis=0).astype(BF16)
        o = jnp.concatenate(o_q[i], axis=0) + _dot(qkd[i], vn_all)
        refs[d][5][:, lo:lo + DN_HEAD_DIM] = o.astype(BF16)


def _delta_call(dq, dk, dv, bg, bgt, seq_len):
    n = dq.shape[0]
    batch = n // seq_len
    nb = seq_len // DN_BLK
    groups = batch // DN_SEQS
    dq, dk, dv = (a.reshape(groups, DN_SEQS, seq_len, DN_W) for a in (dq, dk, dv))
    bg = bg.reshape(groups, DN_SEQS, seq_len, LANES)
    bgt = bgt.reshape(groups, DN_SEQS, 4 * N_DN_HEADS, seq_len)
    fwd = lambda b, i: (b, 0, i, 0)
    bwd = lambda b, i: (b, 0, nb - 1 - i, 0)
    fwd_t = lambda b, i: (b, 0, 0, i)
    bwd_t = lambda b, i: (b, 0, 0, nb - 1 - i)
    tok = lambda m: pl.BlockSpec((1, DN_SEQS, DN_BLK, DN_W), m)
    in_specs = [tok(fwd), tok(fwd), tok(fwd), pl.BlockSpec((1, DN_SEQS, DN_BLK, LANES), fwd),
                pl.BlockSpec((1, DN_SEQS, 4 * N_DN_HEADS, DN_BLK), fwd_t),
                tok(bwd), tok(bwd), tok(bwd), pl.BlockSpec((1, DN_SEQS, DN_BLK, LANES), bwd),
                pl.BlockSpec((1, DN_SEQS, 4 * N_DN_HEADS, DN_BLK), bwd_t)]
    o_shape = jax.ShapeDtypeStruct((groups, DN_SEQS, seq_len, DN_W), BF16)
    o_f, o_b = pl.pallas_call(
        _delta_kernel,
        grid=(groups, nb),
        in_specs=in_specs,
        out_specs=(tok(fwd), tok(bwd)),
        out_shape=(o_shape, o_shape),
        scratch_shapes=[pltpu.VMEM((DN_SEQS * DN_DIRS * N_DN_HEADS // 2, DN_HEAD_DIM, 2 * DN_HEAD_DIM), F32)],
        compiler_params=pltpu.CompilerParams(dimension_semantics=("arbitrary", "arbitrary"),
                                             vmem_limit_bytes=VMEM_LIMIT_BYTES),
        name="delta",
    )(dq, dk, dv, bg, bgt, dq, dk, dv, bg, bgt)
    return o_f.reshape(n, DN_W), o_b.reshape(n, DN_W)


def _out_kernel(x_ref, attn_ref, of_ref, ob_ref, sz_ref, gate_ref, dng_ref, wa_ref, wb_ref, wo_ref,
                g1p_ref, g2_ref, w1_ref, w2_ref, g2p_ref, y_ref):
    o = of_ref[...].astype(F32) + ob_ref[...].astype(F32)
    dng = dng_ref[...]
    dn = jnp.concatenate([_rms(o[:, h * DN_HEAD_DIM:(h + 1) * DN_HEAD_DIM], dng)
                          for h in range(N_DN_HEADS)], axis=1)
    dn = (dn * sz_ref[...].astype(F32)).astype(BF16)
    a = _dot(attn_ref[...], wa_ref[...])
    dd = _dot(dn, wb_ref[...])
    gate = gate_ref[...].astype(F32)
    merged = (gate[:, :D_MODEL] * a + gate[:, D_MODEL:] * dd).astype(BF16)
    h1 = x_ref[...] + _rms(_dot(merged, wo_ref[...]), g1p_ref[...])
    hid = _dot(_rms(h1, g2_ref[...]).astype(BF16), w1_ref[...])
    hid = jnp.square(jnp.maximum(hid, 0.0)).astype(BF16)
    y_ref[...] = h1 + _rms(_dot(hid, w2_ref[...]), g2p_ref[...])


def _out_call(x2d, attn2d, o_f, o_b, sz, gate, dng, wa, wb, wo, g1p, g2, w1, w2, g2p):
    n = x2d.shape[0]
    tm = OUT_TM
    tok = lambda w: pl.BlockSpec((tm, w), lambda i: (i, 0))
    full = lambda a: pl.BlockSpec(a.shape, lambda i: (0, 0), pipeline_mode=pl.Buffered(1))
    return pl.pallas_call(
        _out_kernel,
        grid=(n // tm,),
        in_specs=[tok(D_MODEL), tok(ATT_Q), tok(DN_W), tok(DN_W), tok(DN_W), tok(2 * D_MODEL),
                  full(dng), full(wa), full(wb), full(wo), full(g1p), full(g2), full(w1), full(w2), full(g2p)],
        out_specs=tok(D_MODEL),
        out_shape=jax.ShapeDtypeStruct((n, D_MODEL), F32),
        compiler_params=pltpu.CompilerParams(dimension_semantics=("arbitrary",),
                                             vmem_limit_bytes=VMEM_LIMIT_BYTES),
        name="outmlp",
    )(x2d, attn2d, o_f, o_b, sz, gate, dng, wa, wb, wo, g1p, g2, w1, w2, g2p)


def _rope_table_t(seq_len):
    pos = jnp.arange(seq_len, dtype=jnp.int32)
    row_ids = (pos // GRID_W).astype(F32)
    col_ids = (pos % GRID_W).astype(F32)
    inv_freq = ROPE_THETA ** (-jnp.arange(0, ROPE_HALF, 2, dtype=F32) / ROPE_HALF)
    ang_r = inv_freq[:, None] * row_ids[None, :]
    ang_c = inv_freq[:, None] * col_ids[None, :]
    return jnp.concatenate([jnp.cos(ang_r), jnp.sin(ang_r), jnp.cos(ang_c), jnp.sin(ang_c)], axis=0)


def _pack_w_in(w_in):
    aq, ak, av, dqkv, dz, dbeta, da, gates = jnp.split(
        w_in, [ATT_Q, ATT_Q + ATT_KV, ATT_Q + 2 * ATT_KV, C_Z, C_Z + DN_W,
               C_Z + DN_W + 2 * N_DN_HEADS, C_Z + DN_W + 4 * N_DN_HEADS], axis=-1)
    pad = jnp.zeros((D_MODEL, LANES - 4 * N_DN_HEADS), w_in.dtype)
    return jnp.concatenate([aq, ak, av, dqkv, dz, gates, dbeta, da, pad], axis=-1).astype(BF16)


def _lane_row(v):
    v = v.reshape(-1).astype(F32)
    return jnp.zeros((1, LANES), F32).at[0, 2 * N_DN_HEADS:4 * N_DN_HEADS].set(v)


def _layer(x, p):
    batch, seq_len, _ = x.shape
    n = batch * seq_len
    x2d = x.reshape(n, D_MODEL)
    rope_t = _rope_table_t(seq_len)
    qt, k, vt, dq, dk, dv, sz, gate, bg, bgt = _proj_call(
        x2d, seq_len, p["g1"], p["w_pack"], rope_t, p["qg_t"], p["kg_t"], p["conv_w"], p["alog"], p["dtb"])
    attn = _attn_call(qt, k, vt).reshape(n, ATT_Q)
    o_f, o_b = _delta_call(dq, dk, dv, bg, bgt, seq_len)
    y = _out_call(x2d, attn, o_f, o_b, sz, gate, p["dng"], p["wa"], p["wb"], p["wo"],
                  p["g1p"], p["g2"], p["w1"], p["w2"], p["g2p"])
    return y.reshape(batch, seq_len, D_MODEL)


def kernel(x_prompt, x_sample, ln1_pre_g, w_in, attn_q_norm_g, attn_k_norm_g, dn_conv_w, dn_A_log, dn_dt_bias, dn_out_norm_g, w_attn_branch, w_dn_branch, w_out, ln1_post_g, ln2_pre_g, w_ff_in, w_ff_out, ln2_post_g):
    depth = w_in.shape[0]
    outs = []
    for x in (x_prompt, x_sample):
        for l in range(depth):
            p = {
                "g1": ln1_pre_g[l].reshape(1, D_MODEL),
                "w_pack": _pack_w_in(w_in[l]),
                "qg_t": jnp.broadcast_to(attn_q_norm_g[l][:, None], (HEAD_DIM, PROJ_TM)),
                "kg_t": jnp.broadcast_to(attn_k_norm_g[l][:, None], (HEAD_DIM, PROJ_TM)),
                "conv_w": dn_conv_w[l],
                "alog": _lane_row(dn_A_log[l]),
                "dtb": _lane_row(dn_dt_bias[l]),
                "dng": dn_out_norm_g[l].reshape(1, DN_HEAD_DIM),
                "wa": w_attn_branch[l].astype(BF16),
                "wb": w_dn_branch[l].astype(BF16),
                "wo": w_out[l].astype(BF16),
                "g1p": ln1_post_g[l].reshape(1, D_MODEL),
                "g2": ln2_pre_g[l].reshape(1, D_MODEL),
                "w1": w_ff_in[l].astype(BF16),
                "w2": w_ff_out[l].astype(BF16),
                "g2p": ln2_post_g[l].reshape(1, D_MODEL),
            }
            x = _layer(x, p)
        outs.append(x)
    return tuple(outs)
```
